```python
import math
import jax
import jax.numpy as jnp
from jax import lax
import numpy as np


D_MODEL = 1024
BATCH = 8
SEQ = 8192
DEPTH = 2

MEM_LEN = 256
N_BRANCH = 4
MIX_W = D_MODEL // N_BRANCH
N_IN_SPLITS = 11
IN_COLS = N_IN_SPLITS * MIX_W
HG_HEADS = 4
HG_DH = MIX_W // HG_HEADS
HG_CHUNK = 32
RET_HEADS = 4
RET_DH = MIX_W // RET_HEADS
RET_CHUNK = 128
ROPE_BASE = 10000.0
LRU_BLOCKS = 4
LRU_BS = MIX_W // LRU_BLOCKS
CONV_W = 4
LRU_C = 8.0
S5_GROUP = 16
S5_GROUPS = MIX_W // S5_GROUP
S5_STATE = 64
XA_HEADS = 4
XA_DH = D_MODEL // XA_HEADS
D_FF = -(-8 * D_MODEL // (3 * 256)) * 256
EPS = 1e-6

kernel_name = 'hybrid_gated_hgrn2_retention_rglru_s5_block'


def rms_norm(x, g):
    xf = x.astype(jnp.float32)
    y = xf * lax.rsqrt(jnp.mean(xf * xf, axis=-1, keepdims=True) + EPS)
    return (y * g.astype(jnp.float32)).astype(x.dtype)


def _head_rms(o, n_heads, g):
    B_, S_, W_ = o.shape
    oh = o.reshape(B_, S_, n_heads, W_ // n_heads)
    oh = oh * lax.rsqrt(jnp.mean(oh * oh, axis=-1, keepdims=True) + EPS)
    return oh.reshape(B_, S_, W_) * g.astype(jnp.float32)


def _to_chunks(t, n_heads, chunk):
    B_, S_, W_ = t.shape
    return t.reshape(B_, S_ // chunk, chunk, n_heads, W_ // n_heads).transpose(0, 3, 1, 2, 4)


def _from_chunks(o):
    B_, H_, N_, C_, d_ = o.shape
    return o.transpose(0, 2, 3, 1, 4).reshape(B_, N_ * C_, H_ * d_)


def _inter_chunk(q_dec, k_end, v, decay_end):
    def step(state, inp):
        qn, kn, vn, dn = inp
        o = jnp.einsum('bhck,bhkv->bhcv', qn, state)
        state = dn[..., None] * state + jnp.einsum('bhck,bhcv->bhkv', kn, vn)
        return state, o
    xs = (jnp.moveaxis(q_dec, 2, 0), jnp.moveaxis(k_end, 2, 0),
          jnp.moveaxis(v, 2, 0), jnp.moveaxis(decay_end, 2, 0))
    B_, H_, _, _, K_ = q_dec.shape
    s0 = jnp.zeros((B_, H_, K_, v.shape[-1]), jnp.float32)
    _, o = lax.scan(step, s0, xs)
    return jnp.moveaxis(o, 0, 2)


def hgrn2_mixer(q, f_logit, i, g, lb, norm_g):
    dt = q.dtype
    q, f_logit, i, g = (t.astype(jnp.float32) for t in (q, f_logit, i, g))
    lb = lb.astype(jnp.float32)
    f = lb + (1.0 - lb) * jax.nn.sigmoid(f_logit)
    log_f = jnp.log(f)
    k = 1.0 - f
    q = jax.nn.silu(q)
    qc, kc, vc, lfc = (_to_chunks(t, HG_HEADS, HG_CHUNK) for t in (q, k, i, log_f))
    b = jnp.cumsum(lfc, axis=3)
    b_end = b[..., -1:, :]
    q_dec = qc * jnp.exp(b)
    k_inv = kc * jnp.exp(-b)
    k_end = kc * jnp.exp(b_end - b)
    causal = jnp.tril(jnp.ones((HG_CHUNK, HG_CHUNK), dtype=bool))
    scores = jnp.where(causal, jnp.einsum('bhnck,bhnsk->bhncs', q_dec, k_inv), 0.0)
    o = jnp.einsum('bhncs,bhnsv->bhncv', scores, vc)
    o = o + _inter_chunk(q_dec, k_end, vc, jnp.exp(b_end[..., 0, :]))
    o = _head_rms(_from_chunks(o), HG_HEADS, norm_g) * jax.nn.silu(g)
    return o.astype(dt)


def _rotate(t, cos, sin):
    t1 = t[..., 0::2]
    t2 = t[..., 1::2]
    c = cos[None, :, None, :]
    s = sin[None, :, None, :]
    return jnp.stack([t1 * c - t2 * s, t1 * s + t2 * c], axis=-1).reshape(t.shape)


def retention_mixer(q, k, v, g, norm_g):
    dt = q.dtype
    q, k, v, g = (t.astype(jnp.float32) for t in (q, k, v, g))
    B_, S_, _ = q.shape
    pos = jnp.arange(S_, dtype=jnp.float32)
    inv_freq = ROPE_BASE ** (-jnp.arange(0, RET_DH, 2, dtype=jnp.float32) / RET_DH)
    ang = pos[:, None] * inv_freq[None, :]
    cos, sin = jnp.cos(ang), jnp.sin(ang)
    q = _rotate(q.reshape(B_, S_, RET_HEADS, RET_DH), cos, sin).reshape(B_, S_, MIX_W)
    k = _rotate(k.reshape(B_, S_, RET_HEADS, RET_DH), cos, sin).reshape(B_, S_, MIX_W) * (RET_DH ** -0.5)
    log_gamma = jnp.log1p(-jnp.power(2.0, -5.0 - jnp.arange(RET_HEADS, dtype=jnp.float32)))
    idx = jnp.arange(RET_CHUNK, dtype=jnp.float32)
    rel = idx[:, None] - idx[None, :]
    causal = rel >= 0
    decay = jnp.where(causal, jnp.exp(jnp.where(causal, rel, 0.0)[None] * log_gamma[:, None, None]), 0.0)
    qc, kc, vc = (_to_chunks(t, RET_HEADS, RET_CHUNK) for t in (q, k, v))
    scores = jnp.einsum('bhncd,bhnsd->bhncs', qc, kc) * decay[None, :, None]
    o = jnp.einsum('bhncs,bhnsv->bhncv', scores, vc)
    xi = jnp.exp((idx + 1.0)[None, :] * log_gamma[:, None])
    zeta = jnp.exp((RET_CHUNK - 1.0 - idx)[None, :] * log_gamma[:, None])
    g_end = jnp.exp(RET_CHUNK * log_gamma)
    q_x = qc * xi[None, :, None, :, None]
    k_z = kc * zeta[None, :, None, :, None]
    N_ = qc.shape[2]
    decay_end = jnp.broadcast_to(g_end[None, :, None, None], (B_, RET_HEADS, N_, RET_DH))
    o = o + _inter_chunk(q_x, k_z, vc, decay_end)
    o = _head_rms(_from_chunks(o), RET_HEADS, norm_g) * jax.nn.silu(g)
    return o.astype(dt)


def _linear_combine(e1, e2):
    a1, b1 = e1
    a2, b2 = e2
    return a1 * a2, a2 * b1 + b2


def rglru_mixer(x_gate, x_in, conv_w, conv_b, wa, ba, wx, bx, lam):
    dt = x_in.dtype
    xg = x_gate.astype(jnp.float32)
    xi = x_in.astype(jnp.float32)
    B_, S_, _ = xi.shape
    xc = lax.conv_general_dilated(
        xi, conv_w.astype(jnp.float32)[:, None, :], window_strides=(1,),
        padding=[(CONV_W - 1, 0)], dimension_numbers=('NWC', 'WIO', 'NWC'),
        feature_group_count=MIX_W) + conv_b.astype(jnp.float32)
    xb = xc.reshape(B_, S_, LRU_BLOCKS, LRU_BS)
    r = jax.nn.sigmoid(jnp.einsum('bsni,nij->bsnj', xb, wa.astype(jnp.float32)).reshape(B_, S_, MIX_W) + ba.astype(jnp.float32))
    ig = jax.nn.sigmoid(jnp.einsum('bsni,nij->bsnj', xb, wx.astype(jnp.float32)).reshape(B_, S_, MIX_W) + bx.astype(jnp.float32))
    log_a = -LRU_C * r * jax.nn.softplus(-lam.astype(jnp.float32))
    a = jnp.exp(log_a)
    u = jnp.sqrt(-jnp.expm1(2.0 * log_a)) * (ig * xc)
    _, h = lax.associative_scan(_linear_combine, (a, u), axis=1)
    return (h * jax.nn.gelu(xg)).astype(dt)


def _complex_combine(e1, e2):
    ar1, ai1, br1, bi1 = e1
    ar2, ai2, br2, bi2 = e2
    ar = ar1 * ar2 - ai1 * ai2
    ai = ar1 * ai2 + ai1 * ar2
    br = ar2 * br1 - ai2 * bi1 + br2
    bi = ar2 * bi1 + ai2 * br1 + bi2
    return ar, ai, br, bi


def s5_mixer(u, lam_re, lam_im, b_re, b_im, c_re, c_im, d, log_dt, glu_w, glu_b):
    dt_ = u.dtype
    uf = u.astype(jnp.float32)
    lam_re, lam_im, b_re, b_im, c_re, c_im, d, log_dt = (
        t.astype(jnp.float32) for t in (lam_re, lam_im, b_re, b_im, c_re, c_im, d, log_dt))
    B_, S_, _ = uf.shape
    step = jnp.exp(log_dt)[:, None]
    mag = jnp.exp(lam_re * step)
    lb_re = mag * jnp.cos(lam_im * step)
    lb_im = mag * jnp.sin(lam_im * step)
    den = lam_re * lam_re + lam_im * lam_im
    f_re = ((lb_re - 1.0) * lam_re + lb_im * lam_im) / den
    f_im = (lb_im * lam_re - (lb_re - 1.0) * lam_im) / den
    bb_re = f_re[..., None] * b_re - f_im[..., None] * b_im
    bb_im = f_re[..., None] * b_im + f_im[..., None] * b_re
    ug = uf.reshape(B_, S_, S5_GROUPS, S5_GROUP)
    bu_re = jnp.einsum('bsgp,gnp->bsgn', ug, bb_re)
    bu_im = jnp.einsum('bsgp,gnp->bsgn', ug, bb_im)
    a_re = jnp.broadcast_to(lb_re[None, None], (1, S_, S5_GROUPS, S5_STATE))
    a_im = jnp.broadcast_to(lb_im[None, None], (1, S_, S5_GROUPS, S5_STATE))
    _, _, h_re, h_im = lax.associative_scan(_complex_combine, (a_re, a_im, bu_re, bu_im), axis=1)
    y = jnp.einsum('bsgn,gpn->bsgp', h_re, c_re) - jnp.einsum('bsgn,gpn->bsgp', h_im, c_im)
    y = y.reshape(B_, S_, MIX_W) + d * uf
    act = jax.nn.gelu(y)
    out = act * jax.nn.sigmoid(act @ glu_w.astype(jnp.float32) + glu_b.astype(jnp.float32))
    return out.astype(dt_)


def mixer_block(h, lb, w_in, w_gate, b_gate, hg_norm, ret_norm, conv_w, conv_b, wa, ba, wx, bx, lam,
                lam_re, lam_im, b_re, b_im, c_re, c_im, s5_d, log_dt, glu_w, glu_b, w_up, w_out):
    B_, S_, _ = h.shape
    z = h @ w_in
    hq, hf, hi, hg, rq, rk, rv, rg, lgate, lx, su = jnp.split(z, N_IN_SPLITS, axis=-1)
    y_a = hgrn2_mixer(hq, hf, hi, hg, lb, hg_norm)
    y_b = retention_mixer(rq, rk, rv, rg, ret_norm)
    y_c = rglru_mixer(lgate, lx, conv_w, conv_b, wa, ba, wx, bx, lam)
    y_d = s5_mixer(su, lam_re, lam_im, b_re, b_im, c_re, c_im, s5_d, log_dt, glu_w, glu_b)
    branches = jnp.stack([y_a, y_b, y_c, y_d], axis=2)
    up = jnp.einsum('bsnc,ncd->bsnd', branches, w_up)
    gates = jax.nn.sigmoid(h @ w_gate + b_gate).reshape(B_, S_, N_BRANCH, D_MODEL)
    return jnp.sum(gates * up, axis=2) @ w_out


def cross_attention(h, m, w_q, w_kv, w_o):
    B_, S_, _ = h.shape
    M_ = m.shape[1]
    q = (h @ w_q).reshape(B_, S_, XA_HEADS, XA_DH)
    k, v = jnp.split(m @ w_kv, 2, axis=-1)
    k = k.reshape(B_, M_, XA_HEADS, XA_DH)
    v = v.reshape(B_, M_, XA_HEADS, XA_DH)
    s = jnp.einsum('bshd,bmhd->bhsm', q, k).astype(jnp.float32) * (XA_DH ** -0.5)
    p = jax.nn.softmax(s, axis=-1).astype(v.dtype)
    o = jnp.einsum('bhsm,bmhd->bshd', p, v).reshape(B_, S_, D_MODEL)
    return o @ w_o


def swiglu(h, w_gu, w_down):
    gate, up = jnp.split(h @ w_gu, 2, axis=-1)
    return (jax.nn.silu(gate) * up) @ w_down


def setup_inputs(seed: int = 0) -> dict:
    key = jax.random.key(seed)
    ks = iter(jax.random.split(key, 64))
    f32 = jnp.float32
    L, W, G, N, P = DEPTH, MIX_W, S5_GROUPS, S5_STATE, S5_GROUP

    def nrm(shape, scale):
        return jax.random.normal(next(ks), shape, f32) * scale

    def gain(shape):
        return 1.0 + 0.02 * jax.random.normal(next(ks), shape, f32)

    x = nrm((BATCH, SEQ, D_MODEL), 1.0)
    mem = nrm((BATCH, MEM_LEN, D_MODEL), 1.0)
    hg_lower_bounds = nrm((L, W), 0.5)
    norm_mix_pre = gain((L, D_MODEL))
    norm_mix_post = gain((L, D_MODEL))
    w_in = nrm((L, D_MODEL, IN_COLS), D_MODEL ** -0.5)
    w_gate = nrm((L, D_MODEL, N_BRANCH * D_MODEL), D_MODEL ** -0.5)
    b_gate = nrm((L, N_BRANCH * D_MODEL), 0.02)
    hg_norm = gain((L, W))
    ret_norm = gain((L, W))
    lru_conv_w = nrm((L, CONV_W, W), CONV_W ** -0.5)
    lru_conv_b = nrm((L, W), 0.02)
    lru_wa = nrm((L, LRU_BLOCKS, LRU_BS, LRU_BS), LRU_BS ** -0.5)
    lru_ba = nrm((L, W), 0.02)
    lru_wx = nrm((L, LRU_BLOCKS, LRU_BS, LRU_BS), LRU_BS ** -0.5)
    lru_bx = nrm((L, W), 0.02)
    a_c = jax.random.uniform(next(ks), (L, W), f32, minval=0.9, maxval=0.999)
    a_base = a_c ** (1.0 / LRU_C)
    lru_lambda = jnp.log(a_base) - jnp.log1p(-a_base)
    n_idx = jnp.arange(N, dtype=f32)
    s5_lam_re = -0.5 + 0.01 * jax.random.normal(next(ks), (L, G, N), f32)
    s5_lam_im = math.pi * n_idx[None, None, :] + 0.01 * jax.random.normal(next(ks), (L, G, N), f32)
    s5_b_re = nrm((L, G, N, P), (2.0 * P) ** -0.5)
    s5_b_im = nrm((L, G, N, P), (2.0 * P) ** -0.5)
    s5_c_re = nrm((L, G, P, N), (2.0 * N) ** -0.5 * 4.0)
    s5_c_im = nrm((L, G, P, N), (2.0 * N) ** -0.5 * 4.0)
    s5_d = nrm((L, W), 1.0)
    s5_log_dt = jax.random.uniform(next(ks), (L, G), f32, minval=math.log(1e-3), maxval=math.log(1e-1))
    s5_glu_w = nrm((L, W, W), W ** -0.5)
    s5_glu_b = nrm((L, W), 0.02)
    w_up = nrm((L, N_BRANCH, W, D_MODEL), W ** -0.5)
    w_out = nrm((L, D_MODEL, D_MODEL), D_MODEL ** -0.5)
    norm_xa_pre = gain((L, D_MODEL))
    norm_xa_post = gain((L, D_MODEL))
    norm_mem = gain((L, D_MODEL))
    xa_w_q = nrm((L, D_MODEL, D_MODEL), D_MODEL ** -0.5)
    xa_w_kv = nrm((L, D_MODEL, 2 * D_MODEL), D_MODEL ** -0.5)
    xa_w_o = nrm((L, D_MODEL, D_MODEL), D_MODEL ** -0.5)
    norm_ffn_pre = gain((L, D_MODEL))
    norm_ffn_post = gain((L, D_MODEL))
    ffn_w_gu = nrm((L, D_MODEL, 2 * D_FF), D_MODEL ** -0.5)
    ffn_w_down = nrm((L, D_FF, D_MODEL), D_FF ** -0.5)
    return {
        'x': x, 'mem': mem, 'hg_lower_bounds': hg_lower_bounds,
        'norm_mix_pre': norm_mix_pre, 'norm_mix_post': norm_mix_post,
        'w_in': w_in, 'w_gate': w_gate, 'b_gate': b_gate,
        'hg_norm': hg_norm, 'ret_norm': ret_norm,
        'lru_conv_w': lru_conv_w, 'lru_conv_b': lru_conv_b, 'lru_wa': lru_wa, 'lru_ba': lru_ba,
        'lru_wx': lru_wx, 'lru_bx': lru_bx, 'lru_lambda': lru_lambda,
        's5_lam_re': s5_lam_re, 's5_lam_im': s5_lam_im, 's5_b_re': s5_b_re, 's5_b_im': s5_b_im,
        's5_c_re': s5_c_re, 's5_c_im': s5_c_im, 's5_d': s5_d, 's5_log_dt': s5_log_dt,
        's5_glu_w': s5_glu_w, 's5_glu_b': s5_glu_b,
        'w_up': w_up, 'w_out': w_out,
        'norm_xa_pre': norm_xa_pre, 'norm_xa_post': norm_xa_post, 'norm_mem': norm_mem,
        'xa_w_q': xa_w_q, 'xa_w_kv': xa_w_kv, 'xa_w_o': xa_w_o,
        'norm_ffn_pre': norm_ffn_pre, 'norm_ffn_post': norm_ffn_post,
        'ffn_w_gu': ffn_w_gu, 'ffn_w_down': ffn_w_down,
    }


def reference(x, mem, hg_lower_bounds, norm_mix_pre, norm_mix_post, w_in, w_gate, b_gate,
              hg_norm, ret_norm, lru_conv_w, lru_conv_b, lru_wa, lru_ba, lru_wx, lru_bx, lru_lambda,
              s5_lam_re, s5_lam_im, s5_b_re, s5_b_im, s5_c_re, s5_c_im, s5_d, s5_log_dt,
              s5_glu_w, s5_glu_b, w_up, w_out, norm_xa_pre, norm_xa_post, norm_mem,
              xa_w_q, xa_w_kv, xa_w_o, norm_ffn_pre, norm_ffn_post, ffn_w_gu, ffn_w_down):
    p = jax.nn.softmax(hg_lower_bounds.astype(jnp.float32), axis=0)
    lower_bounds = jnp.cumsum(p, axis=0) - p[0:1]
    for l in range(DEPTH):
        h = rms_norm(x, norm_mix_pre[l])
        y = mixer_block(h, lower_bounds[l], w_in[l], w_gate[l], b_gate[l], hg_norm[l], ret_norm[l],
                        lru_conv_w[l], lru_conv_b[l], lru_wa[l], lru_ba[l], lru_wx[l], lru_bx[l], lru_lambda[l],
                        s5_lam_re[l], s5_lam_im[l], s5_b_re[l], s5_b_im[l], s5_c_re[l], s5_c_im[l],
                        s5_d[l], s5_log_dt[l], s5_glu_w[l], s5_glu_b[l], w_up[l], w_out[l])
        x = x + rms_norm(y, norm_mix_post[l])
        h = rms_norm(x, norm_xa_pre[l])
        m = rms_norm(mem, norm_mem[l])
        x = x + rms_norm(cross_attention(h, m, xa_w_q[l], xa_w_kv[l], xa_w_o[l]), norm_xa_post[l])
        h = rms_norm(x, norm_ffn_pre[l])
        x = x + rms_norm(swiglu(h, ffn_w_gu[l], ffn_w_down[l]), norm_ffn_post[l])
    return x
```

```python
import functools
import math

import jax
import jax.numpy as jnp
from jax import lax
from jax.experimental import pallas as pl
from jax.experimental.pallas import tpu as pltpu

F32 = jnp.float32
BF16 = jnp.bfloat16

EPS = 1e-6
LANES = 128
SUBLANES = 8
VMEM_LIMIT = 56 * 1024 * 1024

MIX_W = 256
HEADS = 4
HEAD_W = MIX_W // HEADS
HG_CHUNK = 32
SEQ_BLOCK = 256
ROPE_BASE = 10000.0
LRU_C = 8.0
CONV_W = 4
S5_GROUPS = 16
S5_P = 16
S5_N = 64
S5_LANES = S5_GROUPS * S5_N
ROW_TILE = 512
SCAN_BLOCK = 128


def _dot(a, b):
    return jnp.dot(a, b, preferred_element_type=F32)


def _dot_nt(a, b):
    return lax.dot_general(a, b, (((1,), (1,)), ((), ())), preferred_element_type=F32)


def _dot_tn(a, b):
    return lax.dot_general(a, b, (((0,), (0,)), ((), ())), preferred_element_type=F32)


def _rms(x, g):
    return x * lax.rsqrt(jnp.mean(x * x, axis=-1, keepdims=True) + EPS) * g


def _silu(x):
    return x * jax.nn.sigmoid(x)


def _gelu_tanh(x):
    return 0.5 * x * (1.0 + jnp.tanh(math.sqrt(2.0 / math.pi) * (x + 0.044715 * (x * x * x))))


def _split3(x):
    hi = x.astype(BF16)
    r1 = x - hi.astype(F32)
    mid = r1.astype(BF16)
    lo = (r1 - mid.astype(F32)).astype(BF16)
    return hi, mid, lo


def _split2(x):
    hi = x.astype(BF16)
    lo = (x - hi.astype(F32)).astype(BF16)
    return hi, lo


def _params(*sem):
    return pltpu.CompilerParams(dimension_semantics=sem, vmem_limit_bytes=VMEM_LIMIT)


def _const_spec(shape):
    nd = len(shape)
    return pl.BlockSpec(shape, lambda *_: (0,) * nd)


def _in_proj_body(splits, x_ref, g_ref, w_ref, *out_refs):
    h = _rms(x_ref[...], g_ref[...]).astype(BF16)
    for o_ref, (c0, c1) in zip(out_refs, splits):
        o_ref[...] = _dot(h, w_ref[:, c0:c1])


def _in_proj(x2, g, w):
    t, d = x2.shape
    widths = (4 * MIX_W, 4 * MIX_W, 2 * MIX_W, MIX_W)
    offs = [0]
    for wd in widths:
        offs.append(offs[-1] + wd)
    splits = tuple((offs[i], offs[i + 1]) for i in range(len(widths)))
    return pl.pallas_call(
        functools.partial(_in_proj_body, splits),
        grid=(t // ROW_TILE,),
        in_specs=[pl.BlockSpec((ROW_TILE, d), lambda i: (i, 0)),
                  _const_spec(g.shape), _const_spec(w.shape)],
        out_specs=[pl.BlockSpec((ROW_TILE, wd), lambda i: (i, 0)) for wd in widths],
        out_shape=[jax.ShapeDtypeStruct((t, wd), F32) for wd in widths],
        compiler_params=_params("arbitrary"),
        name="in_proj",
    )(x2, g, w)


def _post_body(x_ref, y0_ref, y1_ref, y2_ref, y3_ref, gpre_ref, gpost_ref, wg_ref, bg_ref,
               wup_ref, wout_ref, o_ref):
    x = x_ref[...]
    d = x.shape[-1]
    h = _rms(x, gpre_ref[...]).astype(BF16)
    merged = None
    for n, y_ref in enumerate((y0_ref, y1_ref, y2_ref, y3_ref)):
        gate = jax.nn.sigmoid(_dot(h, wg_ref[:, n * d:(n + 1) * d]) + bg_ref[:, n * d:(n + 1) * d])
        term = gate * _dot(y_ref[...], wup_ref[n])
        merged = term if merged is None else merged + term
    out = _dot(merged.astype(BF16), wout_ref[...])
    o_ref[...] = x + _rms(out, gpost_ref[...])


def _post(x2, ys, gpre, gpost, wg, bg, wup, wout):
    t, d = x2.shape
    row = lambda i: (i, 0)
    return pl.pallas_call(
        _post_body,
        grid=(t // ROW_TILE,),
        in_specs=[pl.BlockSpec((ROW_TILE, d), row)]
                 + [pl.BlockSpec((ROW_TILE, MIX_W), row) for _ in ys]
                 + [_const_spec(a.shape) for a in (gpre, gpost, wg, bg, wup, wout)],
        out_specs=pl.BlockSpec((ROW_TILE, d), row),
        out_shape=jax.ShapeDtypeStruct((t, d), F32),
        compiler_params=_params("arbitrary"),
        name="mix_post",
    )(x2, *ys, gpre, gpost, wg, bg, wup, wout)


def _kv_body(m_ref, g_ref, w_ref, k_ref, v_ref):
    d = m_ref.shape[-1]
    m = _rms(m_ref[...], g_ref[...]).astype(BF16)
    k_ref[...] = _dot(m, w_ref[:, 0:d]).astype(BF16)
    v_ref[...] = _dot(m, w_ref[:, d:2 * d]).astype(BF16)


def _kv_proj(mem2, g, w):
    t, d = mem2.shape
    tile = min(ROW_TILE, t)
    row = lambda i: (i, 0)
    return pl.pallas_call(
        _kv_body,
        grid=(t // tile,),
        in_specs=[pl.BlockSpec((tile, d), row), _const_spec(g.shape), _const_spec(w.shape)],
        out_specs=[pl.BlockSpec((tile, d), row), pl.BlockSpec((tile, d), row)],
        out_shape=[jax.ShapeDtypeStruct((t, d), BF16), jax.ShapeDtypeStruct((t, d), BF16)],
        compiler_params=_params("arbitrary"),
        name="kv_proj",
    )(mem2, g, w)


def _xattn_body(x_ref, k_ref, v_ref, gpre_ref, gpost_ref, wq_ref, wo_ref, o_ref):
    x = x_ref[...]
    d = x.shape[-1]
    dh = d // HEADS
    h = _rms(x, gpre_ref[...]).astype(BF16)
    q = _dot(h, wq_ref[...])
    outs = []
    for hd in range(HEADS):
        sl = slice(hd * dh, (hd + 1) * dh)
        s = _dot_nt(q[:, sl].astype(BF16), k_ref[:, sl]) * (dh ** -0.5)
        p = jnp.exp(s - jnp.max(s, axis=-1, keepdims=True))
        p = p / jnp.sum(p, axis=-1, keepdims=True)
        outs.append(_dot(p.astype(BF16), v_ref[:, sl]))
    o = jnp.concatenate(outs, axis=1).astype(BF16)
    o_ref[...] = x + _rms(_dot(o, wo_ref[...]), gpost_ref[...])


def _xattn(x3, k3, v3, gpre, gpost, wq, wo):
    b, s, d = x3.shape
    m = k3.shape[1]
    tok = lambda bi, i: (bi, i, 0)
    mem = lambda bi, i: (bi, 0, 0)
    return pl.pallas_call(
        _xattn_body,
        grid=(b, s // ROW_TILE),
        in_specs=[pl.BlockSpec((None, ROW_TILE, d), tok),
                  pl.BlockSpec((None, m, d), mem), pl.BlockSpec((None, m, d), mem)]
                 + [_const_spec(a.shape) for a in (gpre, gpost, wq, wo)],
        out_specs=pl.BlockSpec((None, ROW_TILE, d), tok),
        out_shape=jax.ShapeDtypeStruct((b, s, d), F32),
        compiler_params=_params("arbitrary", "arbitrary"),
        name="xattn",
    )(x3, k3, v3, gpre, gpost, wq, wo)


def _ffn_body(ff_chunk, x_ref, gpre_ref, gpost_ref, wgu_ref, wdn_ref, o_ref):
    x = x_ref[...]
    d_ff = wdn_ref.shape[0]
    h = _rms(x, gpre_ref[...]).astype(BF16)
    acc = None
    for c0 in range(0, d_ff, ff_chunk):
        gate = _dot(h, wgu_ref[:, c0:c0 + ff_chunk])
        up = _dot(h, wgu_ref[:, d_ff + c0:d_ff + c0 + ff_chunk])
        part = _dot((_silu(gate) * up).astype(BF16), wdn_ref[c0:c0 + ff_chunk, :])
        acc = part if acc is None else acc + part
    o_ref[...] = x + _rms(acc, gpost_ref[...])


def _ffn(x2, gpre, gpost, wgu, wdn):
    t, d = x2.shape
    d_ff = wdn.shape[0]
    ff_chunk = d_ff // 2
    assert ff_chunk % LANES == 0
    row = lambda i: (i, 0)
    return pl.pallas_call(
        functools.partial(_ffn_body, ff_chunk),
        grid=(t // ROW_TILE,),
        in_specs=[pl.BlockSpec((ROW_TILE, d), row)]
                 + [_const_spec(a.shape) for a in (gpre, gpost, wgu, wdn)],
        out_specs=pl.BlockSpec((ROW_TILE, d), row),
        out_shape=jax.ShapeDtypeStruct((t, d), F32),
        compiler_params=_params("arbitrary"),
        name="ffn",
    )(x2, gpre, gpost, wgu, wdn)


def _head_norm_gate(o, zg, ng, e_head):
    hi, lo = _split2(o * o)
    ms = _dot(hi, e_head) + _dot(lo, e_head)
    return o * lax.rsqrt(ms + EPS) * ng * _silu(zg)


def _hgrn2_body(z_ref, lb_ref, ng_ref, le_ref, eh_ref, y_ref, st_ref):
    c = SEQ_BLOCK
    w = MIX_W

    @pl.when(pl.program_id(1) == 0)
    def _():
        st_ref[...] = jnp.zeros_like(st_ref)

    zq = z_ref[:, 0:w]
    zf = z_ref[:, w:2 * w]
    v = z_ref[:, 2 * w:3 * w]
    zg = z_ref[:, 3 * w:4 * w]
    lb = lb_ref[...]
    f = lb + (1.0 - lb) * jax.nn.sigmoid(zf)
    logf = jnp.log(f)
    k = 1.0 - f
    q = _silu(zq)

    le = le_ref[...]
    bb = None
    for part in _split3(logf):
        term = _dot(le, part)
        bb = term if bb is None else bb + term
    b = bb[0:c]
    b_end = bb[c:2 * c]
    q_dec = q * jnp.exp(b)
    k_inv = (k * jnp.exp(-b)).astype(BF16)
    k_end = (k * jnp.exp(b_end - b)).astype(BF16)
    v_bf = v.astype(BF16)

    lane = lax.broadcasted_iota(jnp.int32, (1, w), 1)
    row = lax.broadcasted_iota(jnp.int32, (c, c), 0)
    col = lax.broadcasted_iota(jnp.int32, (c, c), 1)
    shift = HG_CHUNK.bit_length() - 1
    causal = (row >= col) & ((row >> shift) == (col >> shift))
    hshift = HEAD_W.bit_length() - 1

    o = None
    for hd in range(HEADS):
        mh = (lane >> hshift) == hd
        qh = jnp.where(mh, q_dec, 0.0).astype(BF16)
        a = jnp.where(causal, _dot_nt(qh, k_inv), 0.0).astype(BF16)
        term = _dot(a, jnp.where(mh, v, 0.0).astype(BF16))
        o = term if o is None else o + term

    st = st_ref[...]
    head_bd = (row >> hshift) == (col >> hshift)
    q_bf = q_dec.astype(BF16)
    parts = []
    for n in range(c // HG_CHUNK):
        sl = slice(n * HG_CHUNK, (n + 1) * HG_CHUNK)
        parts.append(o[sl] + _dot_nt(q_bf[sl], st.astype(BF16)))
        kv = _dot_tn(v_bf[sl], k_end[sl])
        dec = jnp.exp(b_end[n * HG_CHUNK:n * HG_CHUNK + 1, :])
        st = st * dec + jnp.where(head_bd, kv, 0.0)
    st_ref[...] = st
    o = jnp.concatenate(parts, axis=0)

    y_ref[...] = _head_norm_gate(o, zg, ng_ref[...], eh_ref[...]).astype(BF16)


def _chunk_sum_matrix(c):
    r = jnp.arange(c)
    same = (r[:, None] // HG_CHUNK) == (r[None, :] // HG_CHUNK)
    lower = same & (r[:, None] >= r[None, :])
    return jnp.concatenate([lower, same], axis=0).astype(BF16)


def _head_mean_matrix():
    r = jnp.arange(MIX_W)
    same = (r[:, None] // HEAD_W) == (r[None, :] // HEAD_W)
    return (same.astype(F32) / HEAD_W).astype(BF16)


def _hgrn2(z3, lb, ng):
    b, s, _ = z3.shape
    c = SEQ_BLOCK
    le = _chunk_sum_matrix(c)
    eh = _head_mean_matrix()
    tok = lambda bi, i: (bi, i, 0)
    return pl.pallas_call(
        _hgrn2_body,
        grid=(b, s // c),
        in_specs=[pl.BlockSpec((None, c, 4 * MIX_W), tok)]
                 + [_const_spec(a.shape) for a in (lb, ng, le, eh)],
        out_specs=pl.BlockSpec((None, c, MIX_W), tok),
        out_shape=jax.ShapeDtypeStruct((b, s, MIX_W), BF16),
        scratch_shapes=[pltpu.VMEM((MIX_W, MIX_W), F32)],
        compiler_params=_params("arbitrary", "arbitrary"),
        name="hgrn2",
    )(z3, lb, ng, le, eh)


def _ret_body(z_ref, cos_ref, sin_ref, dec_ref, xi_ref, zeta_ref, gend_ref, ng_ref, eh_ref,
              y_ref, st_ref):
    c = SEQ_BLOCK
    w = MIX_W
    hw = w // 2

    @pl.when(pl.program_id(1) == 0)
    def _():
        st_ref[...] = jnp.zeros_like(st_ref)

    cs = cos_ref[...]
    sn = sin_ref[...]
    q1 = z_ref[:, 0:hw]
    q2 = z_ref[:, hw:w]
    k1 = z_ref[:, w:w + hw]
    k2 = z_ref[:, w + hw:2 * w]
    v = z_ref[:, 2 * w:3 * w]
    zg = z_ref[:, 3 * w:4 * w]
    qr = jnp.concatenate([q1 * cs - q2 * sn, q1 * sn + q2 * cs], axis=1)
    kr = jnp.concatenate([k1 * cs - k2 * sn, k1 * sn + k2 * cs], axis=1) * (HEAD_W ** -0.5)
    kr_bf = kr.astype(BF16)

    lane = lax.broadcasted_iota(jnp.int32, (1, w), 1)
    rshift = (hw // HEADS).bit_length() - 1
    hshift = HEAD_W.bit_length() - 1
    head_rot = (lane & (hw - 1)) >> rshift
    head_nat = lane >> hshift

    o = None
    for hd in range(HEADS):
        qh = jnp.where(head_rot == hd, qr, 0.0).astype(BF16)
        sc = (_dot_nt(qh, kr_bf) * dec_ref[hd]).astype(BF16)
        term = _dot(sc, jnp.where(head_nat == hd, v, 0.0).astype(BF16))
        o = term if o is None else o + term

    st = st_ref[...]
    o = o + _dot_nt((qr * xi_ref[...]).astype(BF16), st.astype(BF16))
    kv = _dot_tn(v.astype(BF16), (kr * zeta_ref[...]).astype(BF16))
    row = lax.broadcasted_iota(jnp.int32, (w, w), 0)
    col = lax.broadcasted_iota(jnp.int32, (w, w), 1)
    head_bd = (row >> hshift) == ((col & (hw - 1)) >> rshift)
    st_ref[...] = st * gend_ref[...] + jnp.where(head_bd, kv, 0.0)

    y_ref[...] = _head_norm_gate(o, zg, ng_ref[...], eh_ref[...]).astype(BF16)


def _rot_lane_head():
    lane = jnp.arange(MIX_W)
    return (lane % (MIX_W // 2)) // (MIX_W // 2 // HEADS)


def _retention(z3, ng):
    b, s, _ = z3.shape
    c = SEQ_BLOCK
    hw = MIX_W // 2
    n_freq = HEAD_W // 2
    pos = jnp.arange(s, dtype=F32)
    inv_freq = ROPE_BASE ** (-jnp.arange(0, HEAD_W, 2, dtype=F32) / HEAD_W)
    ang = pos[:, None] * jnp.tile(inv_freq, hw // n_freq)[None, :]
    cos, sin = jnp.cos(ang), jnp.sin(ang)
    log_gamma = jnp.log1p(-jnp.power(2.0, -5.0 - jnp.arange(HEADS, dtype=F32)))
    idx = jnp.arange(c, dtype=F32)
    rel = idx[:, None] - idx[None, :]
    causal = rel >= 0
    decay = jnp.where(causal, jnp.exp(jnp.where(causal, rel, 0.0)[None] * log_gamma[:, None, None]), 0.0)
    lg_lane = log_gamma[_rot_lane_head()]
    xi = jnp.exp((idx + 1.0)[:, None] * lg_lane[None, :])
    zeta = jnp.exp((c - 1.0 - idx)[:, None] * lg_lane[None, :])
    gend = jnp.exp(c * lg_lane)[None, :]
    eh = _head_mean_matrix()
    tok = lambda bi, i: (bi, i, 0)
    return pl.pallas_call(
        _ret_body,
        grid=(b, s // c),
        in_specs=[pl.BlockSpec((None, c, 4 * MIX_W), tok),
                  pl.BlockSpec((c, hw), lambda bi, i: (i, 0)),
                  pl.BlockSpec((c, hw), lambda bi, i: (i, 0))]
                 + [_const_spec(a.shape) for a in (decay, xi, zeta, gend, ng, eh)],
        out_specs=pl.BlockSpec((None, c, MIX_W), tok),
        out_shape=jax.ShapeDtypeStruct((b, s, MIX_W), BF16),
        scratch_shapes=[pltpu.VMEM((MIX_W, MIX_W), F32)],
        compiler_params=_params("arbitrary", "arbitrary"),
        name="retention",
    )(z3, cos, sin, decay, xi, zeta, gend, ng, eh)


def _to_time_major(src_ref, slab_ref, n_slabs):
    nb, sb, _ = src_ref.shape
    for bi in range(nb):
        for ks in range(n_slabs):
            slab_ref[ks, pl.ds(bi, sb, stride=nb), :] = src_ref[bi, :, ks * LANES:(ks + 1) * LANES]


def _from_time_major(slab_ref, dst_ref, n_slabs):
    nb, sb, _ = dst_ref.shape
    for bi in range(nb):
        cols = [slab_ref[ks, pl.ds(bi, sb, stride=nb), :] for ks in range(n_slabs)]
        dst_ref[bi, :, :] = jnp.concatenate(cols, axis=1).astype(dst_ref.dtype)


def _lru_body(z_ref, cw_ref, cb_ref, wbd_ref, bias_ref, c_ref, y_ref,
              slab_ref, a_ref, u_ref, tail_ref, h_ref):
    nb, sb, _ = z_ref.shape
    n = nb * sb
    w = MIX_W
    halo = (CONV_W - 1) * nb

    @pl.when(pl.program_id(0) == 0)
    def _():
        tail_ref[...] = jnp.zeros_like(tail_ref)
        h_ref[...] = jnp.zeros_like(h_ref)

    _to_time_major(z_ref, slab_ref, 2 * w // LANES)
    xg = jnp.concatenate([slab_ref[0], slab_ref[1]], axis=1)
    xi = jnp.concatenate([slab_ref[2], slab_ref[3]], axis=1)

    xpad = jnp.concatenate([tail_ref[...], xi], axis=0)
    xc = cb_ref[...]
    for j in range(CONV_W):
        xc = xc + cw_ref[j:j + 1, :] * xpad[j * nb:j * nb + n]
    tail_ref[...] = xi[n - halo:n]

    gates = jax.nn.sigmoid(_dot(xc.astype(BF16), wbd_ref[...]) + bias_ref[...])
    r = gates[:, 0:w]
    ig = gates[:, w:2 * w]
    a = jnp.exp(c_ref[...] * r)
    a_ref[...] = a
    u_ref[...] = jnp.sqrt(1.0 - a * a) * (ig * xc)

    def step(t, h):
        i = pl.multiple_of(t * nb, nb)
        h = a_ref[pl.ds(i, nb), :] * h + u_ref[pl.ds(i, nb), :]
        u_ref[pl.ds(i, nb), :] = h
        return h

    h_ref[...] = lax.fori_loop(0, sb, step, h_ref[...], unroll=8)

    out = u_ref[...] * _gelu_tanh(xg)
    slab_ref[0] = out[:, 0:LANES]
    slab_ref[1] = out[:, LANES:2 * LANES]
    _from_time_major(slab_ref, y_ref, w // LANES)


def _rglru(z3, cw, cb, wbd, bias, c_row):
    nb, s, _ = z3.shape
    sb = SCAN_BLOCK
    n = nb * sb
    assert nb == SUBLANES
    blk = lambda i: (0, i, 0)
    return pl.pallas_call(
        _lru_body,
        grid=(s // sb,),
        in_specs=[pl.BlockSpec((nb, sb, 2 * MIX_W), blk)]
                 + [_const_spec(a.shape) for a in (cw, cb, wbd, bias, c_row)],
        out_specs=pl.BlockSpec((nb, sb, MIX_W), blk),
        out_shape=jax.ShapeDtypeStruct((nb, s, MIX_W), BF16),
        scratch_shapes=[pltpu.VMEM((2 * MIX_W // LANES, n, LANES), F32),
                        pltpu.VMEM((n, MIX_W), F32),
                        pltpu.VMEM((n, MIX_W), F32),
                        pltpu.VMEM(((CONV_W - 1) * nb, MIX_W), F32),
                        pltpu.VMEM((nb, MIX_W), F32)],
        compiler_params=_params("arbitrary"),
        name="rglru",
    )(z3, cw, cb, wbd, bias, c_row)


def _s5_body(z_ref, lre_ref, lim_ref, bd_ref, cm_ref, d_ref, gw_ref, gb_ref, y_ref,
             slab_ref, bu_ref, hre_ref, him_ref):
    nb, sb, _ = z_ref.shape
    w = MIX_W
    ns = S5_LANES

    @pl.when(pl.program_id(0) == 0)
    def _():
        hre_ref[...] = jnp.zeros_like(hre_ref)
        him_ref[...] = jnp.zeros_like(him_ref)

    _to_time_major(z_ref, slab_ref, w // LANES)
    u = jnp.concatenate([slab_ref[0], slab_ref[1]], axis=1)
    bu_ref[...] = _dot(u.astype(BF16), bd_ref[...])

    lre = jnp.broadcast_to(lre_ref[...], (nb, ns))
    lim = jnp.broadcast_to(lim_ref[...], (nb, ns))

    def step(t, carry):
        hre, him = carry
        i = pl.multiple_of(t * nb, nb)
        nre = lre * hre - lim * him + bu_ref[pl.ds(i, nb), 0:ns]
        nim = lre * him + lim * hre + bu_ref[pl.ds(i, nb), ns:2 * ns]
        bu_ref[pl.ds(i, nb), 0:ns] = nre
        bu_ref[pl.ds(i, nb), ns:2 * ns] = nim
        return nre, nim

    hre, him = lax.fori_loop(0, sb, step, (hre_ref[...], him_ref[...]), unroll=2)
    hre_ref[...] = hre
    him_ref[...] = him

    y = _dot(bu_ref[...].astype(BF16), cm_ref[...]) + d_ref[...] * u
    act = _gelu_tanh(y)
    out = act * jax.nn.sigmoid(_dot(act.astype(BF16), gw_ref[...]) + gb_ref[...])
    slab_ref[0] = out[:, 0:LANES]
    slab_ref[1] = out[:, LANES:2 * LANES]
    _from_time_major(slab_ref, y_ref, w // LANES)


def _s5(z3, lre, lim, bd, cm, d_row, gw, gb):
    nb, s, _ = z3.shape
    sb = SCAN_BLOCK
    n = nb * sb
    assert nb == SUBLANES
    blk = lambda i: (0, i, 0)
    return pl.pallas_call(
        _s5_body,
        grid=(s // sb,),
        in_specs=[pl.BlockSpec((nb, sb, MIX_W), blk)]
                 + [_const_spec(a.shape) for a in (lre, lim, bd, cm, d_row, gw, gb)],
        out_specs=pl.BlockSpec((nb, sb, MIX_W), blk),
        out_shape=jax.ShapeDtypeStruct((nb, s, MIX_W), BF16),
        scratch_shapes=[pltpu.VMEM((MIX_W // LANES, n, LANES), F32),
                        pltpu.VMEM((n, 2 * S5_LANES), F32),
                        pltpu.VMEM((nb, S5_LANES), F32),
                        pltpu.VMEM((nb, S5_LANES), F32)],
        compiler_params=_params("arbitrary"),
        name="s5",
    )(z3, lre, lim, bd, cm, d_row, gw, gb)


def _block_diag(blocks):
    g, r, c = blocks.shape
    eye = jnp.eye(g, dtype=blocks.dtype)
    return jnp.einsum('gh,grc->grhc', eye, blocks).reshape(g * r, g * c)


def _rotary_column_perm():
    e = jnp.arange(2)[:, None, None]
    h = jnp.arange(HEADS)[None, :, None]
    i = jnp.arange(HEAD_W // 2)[None, None, :]
    return (h * HEAD_W + 2 * i + e).reshape(-1)


def _prep_w_in(w_in):
    perm = _rotary_column_perm()
    base = 4 * MIX_W
    cols = jnp.arange(w_in.shape[1])
    cols = cols.at[base:base + MIX_W].set(base + perm)
    cols = cols.at[base + MIX_W:base + 2 * MIX_W].set(base + MIX_W + perm)
    return w_in[:, cols].astype(BF16)


def _prep_s5(lam_re, lam_im, b_re, b_im, c_re, c_im, log_dt):
    step = jnp.exp(log_dt)[:, None]
    mag = jnp.exp(lam_re * step)
    lb_re = mag * jnp.cos(lam_im * step)
    lb_im = mag * jnp.sin(lam_im * step)
    den = lam_re * lam_re + lam_im * lam_im
    f_re = ((lb_re - 1.0) * lam_re + lb_im * lam_im) / den
    f_im = (lb_im * lam_re - (lb_re - 1.0) * lam_im) / den
    bb_re = f_re[..., None] * b_re - f_im[..., None] * b_im
    bb_im = f_re[..., None] * b_im + f_im[..., None] * b_re
    bd = jnp.concatenate([_block_diag(bb_re.transpose(0, 2, 1)),
                          _block_diag(bb_im.transpose(0, 2, 1))], axis=1)
    cm = jnp.concatenate([_block_diag(c_re.transpose(0, 2, 1)),
                          -_block_diag(c_im.transpose(0, 2, 1))], axis=0)
    return (lb_re.reshape(1, -1), lb_im.reshape(1, -1), bd.astype(BF16), cm.astype(BF16))


def _row(v):
    return v.reshape(1, -1).astype(F32)


def kernel(x, mem, hg_lower_bounds, norm_mix_pre, norm_mix_post, w_in, w_gate, b_gate, hg_norm, ret_norm, lru_conv_w, lru_conv_b, lru_wa, lru_ba, lru_wx, lru_bx, lru_lambda, s5_lam_re, s5_lam_im, s5_b_re, s5_b_im, s5_c_re, s5_c_im, s5_d, s5_log_dt, s5_glu_w, s5_glu_b, w_up, w_out, norm_xa_pre, norm_xa_post, norm_mem, xa_w_q, xa_w_kv, xa_w_o, norm_ffn_pre, norm_ffn_post, ffn_w_gu, ffn_w_down):
    b, s, d = x.shape
    depth = w_in.shape[0]
    t = b * s
    p = jax.nn.softmax(hg_lower_bounds.astype(F32), axis=0)
    lower_bounds = jnp.cumsum(p, axis=0) - p[0:1]
    mem2 = mem.reshape(-1, d)

    x2 = x.reshape(t, d)
    for l in range(depth):
        z_hg, z_ret, z_lru, z_s5 = _in_proj(x2, _row(norm_mix_pre[l]), _prep_w_in(w_in[l]))
        y_a = _hgrn2(z_hg.reshape(b, s, -1), _row(lower_bounds[l]), _row(hg_norm[l]))
        y_b = _retention(z_ret.reshape(b, s, -1), _row(ret_norm[l]))
        wbd = jnp.concatenate([_block_diag(lru_wa[l]), _block_diag(lru_wx[l])], axis=1).astype(BF16)
        bias = jnp.concatenate([lru_ba[l], lru_bx[l]]).reshape(1, -1)
        c_row = _row(-LRU_C * jax.nn.softplus(-lru_lambda[l].astype(F32)))
        y_c = _rglru(z_lru.reshape(b, s, -1), lru_conv_w[l], _row(lru_conv_b[l]), wbd, bias, c_row)
        lre, lim, bd, cm = _prep_s5(s5_lam_re[l], s5_lam_im[l], s5_b_re[l], s5_b_im[l],
                                    s5_c_re[l], s5_c_im[l], s5_log_dt[l])
        y_d = _s5(z_s5.reshape(b, s, -1), lre, lim, bd, cm, _row(s5_d[l]),
                  s5_glu_w[l].astype(BF16), _row(s5_glu_b[l]))
        ys = [y.reshape(t, MIX_W) for y in (y_a, y_b, y_c, y_d)]
        x2 = _post(x2, ys, _row(norm_mix_pre[l]), _row(norm_mix_post[l]), w_gate[l].astype(BF16),
                   _row(b_gate[l]), w_up[l].astype(BF16), w_out[l].astype(BF16))

        k2, v2 = _kv_proj(mem2, _row(norm_mem[l]), xa_w_kv[l].astype(BF16))
        x3 = _xattn(x2.reshape(b, s, d), k2.reshape(b, -1, d), v2.reshape(b, -1, d),
                    _row(norm_xa_pre[l]), _row(norm_xa_post[l]),
                    xa_w_q[l].astype(BF16), xa_w_o[l].astype(BF16))
        x2 = _ffn(x3.reshape(t, d), _row(norm_ffn_pre[l]), _row(norm_ffn_post[l]),
                  ffn_w_gu[l].astype(BF16), ffn_w_down[l].astype(BF16))
    return x2.reshape(b, s, d)
```

```python
import functools
import math

import jax
import jax.numpy as jnp
from jax import lax
from jax.experimental import pallas as pl
from jax.experimental.pallas import tpu as pltpu

F32 = jnp.float32
BF16 = jnp.bfloat16

EPS = 1e-6
LANES = 128
SUBLANES = 8
MXU_DIM = 256
VMEM_LIMIT = 56 * 1024 * 1024

MIX_W = 256
HEADS = 4
HEAD_W = MIX_W // HEADS
HG_CHUNK = 32
SEQ_BLOCK = 256
HG_BATCH_GROUP = 2
RET_BATCH_GROUP = 4
ROPE_BASE = 10000.0
LRU_C = 8.0
CONV_W = 4
S5_GROUPS = 16
S5_P = 16
S5_N = 64
S5_LANES = S5_GROUPS * S5_N
ROW_TILE = 512
SCAN_BLOCK = 128


def _dot(a, b):
    return jnp.dot(a, b, preferred_element_type=F32)


def _dot_nt(a, b):
    return lax.dot_general(a, b, (((1,), (1,)), ((), ())), preferred_element_type=F32)


def _dot_tn(a, b):
    return lax.dot_general(a, b, (((0,), (0,)), ((), ())), preferred_element_type=F32)


def _rms(x, g):
    return x * lax.rsqrt(jnp.mean(x * x, axis=-1, keepdims=True) + EPS) * g


def _silu(x):
    return x * jax.nn.sigmoid(x)


def _gelu_tanh(x):
    return 0.5 * x * (1.0 + jnp.tanh(math.sqrt(2.0 / math.pi) * (x + 0.044715 * (x * x * x))))


def _split2(x):
    hi = x.astype(BF16)
    lo = (x - hi.astype(F32)).astype(BF16)
    return hi, lo


def _params(*sem):
    return pltpu.CompilerParams(dimension_semantics=sem, vmem_limit_bytes=VMEM_LIMIT)


def _const_spec(shape):
    nd = len(shape)
    return pl.BlockSpec(shape, lambda *_: (0,) * nd)


def _in_proj_body(splits, x_ref, g_ref, w_ref, *out_refs):
    h = _rms(x_ref[...], g_ref[...]).astype(BF16)
    for o_ref, (c0, c1) in zip(out_refs, splits):
        o_ref[...] = _dot(h, w_ref[:, c0:c1])


def _in_proj(x2, g, w):
    t, d = x2.shape
    widths = (4 * MIX_W, 4 * MIX_W, 2 * MIX_W, MIX_W)
    offs = [0]
    for wd in widths:
        offs.append(offs[-1] + wd)
    splits = tuple((offs[i], offs[i + 1]) for i in range(len(widths)))
    return pl.pallas_call(
        functools.partial(_in_proj_body, splits),
        grid=(t // ROW_TILE,),
        in_specs=[pl.BlockSpec((ROW_TILE, d), lambda i: (i, 0)),
                  _const_spec(g.shape), _const_spec(w.shape)],
        out_specs=[pl.BlockSpec((ROW_TILE, wd), lambda i: (i, 0)) for wd in widths],
        out_shape=[jax.ShapeDtypeStruct((t, wd), F32) for wd in widths],
        compiler_params=_params("arbitrary"),
        name="in_proj",
    )(x2, g, w)


def _post_body(x_ref, y0_ref, y1_ref, y2_ref, y3_ref, gpre_ref, gpost_ref, wg_ref, bg_ref,
               wup_ref, wout_ref, o_ref):
    x = x_ref[...]
    d = x.shape[-1]
    h = _rms(x, gpre_ref[...]).astype(BF16)
    merged = None
    for n, y_ref in enumerate((y0_ref, y1_ref, y2_ref, y3_ref)):
        gate = jax.nn.sigmoid(_dot(h, wg_ref[:, n * d:(n + 1) * d]) + bg_ref[:, n * d:(n + 1) * d])
        term = gate * _dot(y_ref[...], wup_ref[n])
        merged = term if merged is None else merged + term
    out = _dot(merged.astype(BF16), wout_ref[...])
    o_ref[...] = x + _rms(out, gpost_ref[...])


def _post(x2, ys, gpre, gpost, wg, bg, wup, wout):
    t, d = x2.shape
    row = lambda i: (i, 0)
    return pl.pallas_call(
        _post_body,
        grid=(t // ROW_TILE,),
        in_specs=[pl.BlockSpec((ROW_TILE, d), row)]
                 + [pl.BlockSpec((ROW_TILE, MIX_W), row) for _ in ys]
                 + [_const_spec(a.shape) for a in (gpre, gpost, wg, bg, wup, wout)],
        out_specs=pl.BlockSpec((ROW_TILE, d), row),
        out_shape=jax.ShapeDtypeStruct((t, d), F32),
        compiler_params=_params("arbitrary"),
        name="mix_post",
    )(x2, *ys, gpre, gpost, wg, bg, wup, wout)


def _kv_body(m_ref, g_ref, w_ref, k_ref, v_ref):
    d = m_ref.shape[-1]
    m = _rms(m_ref[...], g_ref[...]).astype(BF16)
    k_ref[...] = _dot(m, w_ref[:, 0:d]).astype(BF16)
    v_ref[...] = _dot(m, w_ref[:, d:2 * d]).astype(BF16)


def _kv_proj(mem2, g, w):
    t, d = mem2.shape
    tile = min(ROW_TILE, t)
    row = lambda i: (i, 0)
    return pl.pallas_call(
        _kv_body,
        grid=(t // tile,),
        in_specs=[pl.BlockSpec((tile, d), row), _const_spec(g.shape), _const_spec(w.shape)],
        out_specs=[pl.BlockSpec((tile, d), row), pl.BlockSpec((tile, d), row)],
        out_shape=[jax.ShapeDtypeStruct((t, d), BF16), jax.ShapeDtypeStruct((t, d), BF16)],
        compiler_params=_params("arbitrary"),
        name="kv_proj",
    )(mem2, g, w)


def _xattn_body(x_ref, k_ref, v_ref, gpre_ref, gpost_ref, wq_ref, wo_ref, o_ref):
    x = x_ref[...]
    d = x.shape[-1]
    dh = d // HEADS
    h = _rms(x, gpre_ref[...]).astype(BF16)
    q = _dot(h, wq_ref[...])
    outs = []
    for hd in range(HEADS):
        sl = slice(hd * dh, (hd + 1) * dh)
        s = _dot_nt(q[:, sl].astype(BF16), k_ref[:, sl]) * (dh ** -0.5)
        p = jnp.exp(s - jnp.max(s, axis=-1, keepdims=True))
        p = p / jnp.sum(p, axis=-1, keepdims=True)
        outs.append(_dot(p.astype(BF16), v_ref[:, sl]))
    o = jnp.concatenate(outs, axis=1).astype(BF16)
    o_ref[...] = x + _rms(_dot(o, wo_ref[...]), gpost_ref[...])


def _xattn(x3, k3, v3, gpre, gpost, wq, wo):
    b, s, d = x3.shape
    m = k3.shape[1]
    tok = lambda bi, i: (bi, i, 0)
    mem = lambda bi, i: (bi, 0, 0)
    return pl.pallas_call(
        _xattn_body,
        grid=(b, s // ROW_TILE),
        in_specs=[pl.BlockSpec((None, ROW_TILE, d), tok),
                  pl.BlockSpec((None, m, d), mem), pl.BlockSpec((None, m, d), mem)]
                 + [_const_spec(a.shape) for a in (gpre, gpost, wq, wo)],
        out_specs=pl.BlockSpec((None, ROW_TILE, d), tok),
        out_shape=jax.ShapeDtypeStruct((b, s, d), F32),
        compiler_params=_params("arbitrary", "arbitrary"),
        name="xattn",
    )(x3, k3, v3, gpre, gpost, wq, wo)


def _ffn_body(ff_edges, x_ref, gpre_ref, gpost_ref, wgu_ref, wdn_ref, o_ref):
    x = x_ref[...]
    d_ff = wdn_ref.shape[0]
    h = _rms(x, gpre_ref[...]).astype(BF16)
    acc = None
    for c0, c1 in zip(ff_edges[:-1], ff_edges[1:]):
        gate = _dot(h, wgu_ref[:, c0:c1])
        up = _dot(h, wgu_ref[:, d_ff + c0:d_ff + c1])
        part = _dot((_silu(gate) * up).astype(BF16), wdn_ref[c0:c1, :])
        acc = part if acc is None else acc + part
    o_ref[...] = x + _rms(acc, gpost_ref[...])


def _ffn(x2, gpre, gpost, wgu, wdn):
    t, d = x2.shape
    d_ff = wdn.shape[0]
    n_tiles = d_ff // MXU_DIM
    assert n_tiles * MXU_DIM == d_ff
    ff_edges = (0, (n_tiles + 1) // 2 * MXU_DIM, d_ff)
    row = lambda i: (i, 0)
    return pl.pallas_call(
        functools.partial(_ffn_body, ff_edges),
        grid=(t // ROW_TILE,),
        in_specs=[pl.BlockSpec((ROW_TILE, d), row)]
                 + [_const_spec(a.shape) for a in (gpre, gpost, wgu, wdn)],
        out_specs=pl.BlockSpec((ROW_TILE, d), row),
        out_shape=jax.ShapeDtypeStruct((t, d), F32),
        compiler_params=_params("arbitrary"),
        name="ffn",
    )(x2, gpre, gpost, wgu, wdn)


def _head_norm_gate(o, zg, ng, e_head):
    ms = _dot((o * o).astype(BF16), e_head)
    return o * lax.rsqrt(ms + EPS) * ng * _silu(zg)


def _hgrn2_body(z_ref, lb_ref, ng_ref, le_ref, eh_ref, y_ref, st_ref):
    @pl.when(pl.program_id(1) == 0)
    def _():
        st_ref[...] = jnp.zeros_like(st_ref)

    for bi in range(z_ref.shape[0]):
        _hgrn2_one(z_ref.at[bi], lb_ref, ng_ref, le_ref, eh_ref, y_ref.at[bi], st_ref.at[bi])


def _hgrn2_one(z_ref, lb_ref, ng_ref, le_ref, eh_ref, y_ref, st_ref):
    c = SEQ_BLOCK
    w = MIX_W

    zq = z_ref[:, 0:w]
    zf = z_ref[:, w:2 * w]
    v = z_ref[:, 2 * w:3 * w]
    zg = z_ref[:, 3 * w:4 * w]
    lb = lb_ref[...]
    f = lb + (1.0 - lb) * jax.nn.sigmoid(zf)
    logf = jnp.log(f)
    k = 1.0 - f
    q = _silu(zq)

    le = le_ref[...]
    hi, lo = _split2(logf)
    b = _dot(le, hi) + _dot(le, lo)
    n_chunks = c // HG_CHUNK
    b_last = b.reshape(n_chunks, HG_CHUNK, w)[:, HG_CHUNK - 1:HG_CHUNK, :]
    b_end = jnp.broadcast_to(b_last, (n_chunks, HG_CHUNK, w)).reshape(c, w)
    q_dec = q * jnp.exp(b)
    k_inv = (k * jnp.exp(-b)).astype(BF16)
    k_end = (k * jnp.exp(b_end - b)).astype(BF16)
    v_bf = v.astype(BF16)

    lane = lax.broadcasted_iota(jnp.int32, (1, w), 1)
    row = lax.broadcasted_iota(jnp.int32, (c, c), 0)
    col = lax.broadcasted_iota(jnp.int32, (c, c), 1)
    shift = HG_CHUNK.bit_length() - 1
    causal = (row >= col) & ((row >> shift) == (col >> shift))
    hshift = HEAD_W.bit_length() - 1

    o = None
    for hd in range(HEADS):
        mh = (lane >> hshift) == hd
        qh = jnp.where(mh, q_dec, 0.0).astype(BF16)
        a = jnp.where(causal, _dot_nt(qh, k_inv), 0.0).astype(BF16)
        term = _dot(a, jnp.where(mh, v, 0.0).astype(BF16))
        o = term if o is None else o + term

    st = st_ref[...]
    head_bd = (row >> hshift) == (col >> hshift)
    q_bf = q_dec.astype(BF16)
    vt_bf = v.T.astype(BF16)
    kvs = []
    for n in range(n_chunks):
        sl = slice(n * HG_CHUNK, (n + 1) * HG_CHUNK)
        pieces = [k_end[sl]]
        if n > 0:
            pieces.insert(0, jnp.zeros((n * HG_CHUNK, w), BF16))
        if (n + 1) * HG_CHUNK < c:
            pieces.append(jnp.zeros((c - (n + 1) * HG_CHUNK, w), BF16))
        kvs.append(_dot(vt_bf, jnp.concatenate(pieces, axis=0)))
    states = []
    for n in range(n_chunks):
        states.append(st.astype(BF16))
        dec = jnp.exp(b_end[n * HG_CHUNK:n * HG_CHUNK + 1, :])
        st = st * dec + jnp.where(head_bd, kvs[n], 0.0)
    st_ref[...] = st
    parts = []
    for n in range(n_chunks):
        sl = slice(n * HG_CHUNK, (n + 1) * HG_CHUNK)
        parts.append(o[sl] + _dot_nt(q_bf[sl], states[n]))
    o = jnp.concatenate(parts, axis=0)

    y_ref[...] = _head_norm_gate(o, zg, ng_ref[...], eh_ref[...]).astype(BF16)


def _chunk_sum_matrix(c):
    r = jnp.arange(c)
    same = (r[:, None] // HG_CHUNK) == (r[None, :] // HG_CHUNK)
    lower = same & (r[:, None] >= r[None, :])
    return lower.astype(BF16)


def _head_mean_matrix():
    r = jnp.arange(MIX_W)
    same = (r[:, None] // HEAD_W) == (r[None, :] // HEAD_W)
    return (same.astype(F32) / HEAD_W).astype(BF16)


def _hgrn2(z3, lb, ng):
    b, s, _ = z3.shape
    c = SEQ_BLOCK
    le = _chunk_sum_matrix(c)
    eh = _head_mean_matrix()
    nb = HG_BATCH_GROUP
    tok = lambda bi, i: (bi, i, 0)
    return pl.pallas_call(
        _hgrn2_body,
        grid=(b // nb, s // c),
        in_specs=[pl.BlockSpec((nb, c, 4 * MIX_W), tok)]
                 + [_const_spec(a.shape) for a in (lb, ng, le, eh)],
        out_specs=pl.BlockSpec((nb, c, MIX_W), tok),
        out_shape=jax.ShapeDtypeStruct((b, s, MIX_W), BF16),
        scratch_shapes=[pltpu.VMEM((nb, MIX_W, MIX_W), F32)],
        compiler_params=_params("arbitrary", "arbitrary"),
        name="hgrn2",
    )(z3, lb, ng, le, eh)


def _ret_body(z_ref, cos_ref, sin_ref, dec_ref, xi_ref, zeta_ref, gend_ref, ng_ref, eh_ref,
              y_ref, st_ref):
    @pl.when(pl.program_id(1) == 0)
    def _():
        st_ref[...] = jnp.zeros_like(st_ref)

    for bi in range(z_ref.shape[0]):
        _ret_one(z_ref.at[bi], cos_ref, sin_ref, dec_ref, xi_ref, zeta_ref, gend_ref, ng_ref, eh_ref,
                 y_ref.at[bi], st_ref.at[bi])


def _ret_one(z_ref, cos_ref, sin_ref, dec_ref, xi_ref, zeta_ref, gend_ref, ng_ref, eh_ref,
             y_ref, st_ref):
    c = SEQ_BLOCK
    w = MIX_W
    hw = w // 2

    cs = cos_ref[...]
    sn = sin_ref[...]
    q1 = z_ref[:, 0:hw]
    q2 = z_ref[:, hw:w]
    k1 = z_ref[:, w:w + hw]
    k2 = z_ref[:, w + hw:2 * w]
    v = z_ref[:, 2 * w:3 * w]
    zg = z_ref[:, 3 * w:4 * w]
    qr = jnp.concatenate([q1 * cs - q2 * sn, q1 * sn + q2 * cs], axis=1)
    kr = jnp.concatenate([k1 * cs - k2 * sn, k1 * sn + k2 * cs], axis=1) * (HEAD_W ** -0.5)
    kr_bf = kr.astype(BF16)

    lane = lax.broadcasted_iota(jnp.int32, (1, w), 1)
    rshift = (hw // HEADS).bit_length() - 1
    hshift = HEAD_W.bit_length() - 1
    head_rot = (lane & (hw - 1)) >> rshift
    head_nat = lane >> hshift

    o = None
    for hd in range(HEADS):
        qh = jnp.where(head_rot == hd, qr, 0.0).astype(BF16)
        sc = (_dot_nt(qh, kr_bf) * dec_ref[hd]).astype(BF16)
        term = _dot(sc, jnp.where(head_nat == hd, v, 0.0).astype(BF16))
        o = term if o is None else o + term

    st = st_ref[...]
    o = o + _dot_nt((qr * xi_ref[...]).astype(BF16), st.astype(BF16))
    kv = _dot_tn(v.astype(BF16), (kr * zeta_ref[...]).astype(BF16))
    row = lax.broadcasted_iota(jnp.int32, (w, w), 0)
    col = lax.broadcasted_iota(jnp.int32, (w, w), 1)
    head_bd = (row >> hshift) == ((col & (hw - 1)) >> rshift)
    st_ref[...] = st * gend_ref[...] + jnp.where(head_bd, kv, 0.0)

    y_ref[...] = _head_norm_gate(o, zg, ng_ref[...], eh_ref[...]).astype(BF16)


def _rot_lane_head():
    lane = jnp.arange(MIX_W)
    return (lane % (MIX_W // 2)) // (MIX_W // 2 // HEADS)


def _retention(z3, ng):
    b, s, _ = z3.shape
    c = SEQ_BLOCK
    hw = MIX_W // 2
    n_freq = HEAD_W // 2
    pos = jnp.arange(s, dtype=F32)
    inv_freq = ROPE_BASE ** (-jnp.arange(0, HEAD_W, 2, dtype=F32) / HEAD_W)
    ang = pos[:, None] * jnp.tile(inv_freq, hw // n_freq)[None, :]
    cos, sin = jnp.cos(ang), jnp.sin(ang)
    log_gamma = jnp.log1p(-jnp.power(2.0, -5.0 - jnp.arange(HEADS, dtype=F32)))
    idx = jnp.arange(c, dtype=F32)
    rel = idx[:, None] - idx[None, :]
    causal = rel >= 0
    decay = jnp.where(causal, jnp.exp(jnp.where(causal, rel, 0.0)[None] * log_gamma[:, None, None]), 0.0)
    lg_lane = log_gamma[_rot_lane_head()]
    xi = jnp.exp((idx + 1.0)[:, None] * lg_lane[None, :])
    zeta = jnp.exp((c - 1.0 - idx)[:, None] * lg_lane[None, :])
    gend = jnp.exp(c * lg_lane)[None, :]
    eh = _head_mean_matrix()
    nb = RET_BATCH_GROUP
    tok = lambda bi, i: (bi, i, 0)
    return pl.pallas_call(
        _ret_body,
        grid=(b // nb, s // c),
        in_specs=[pl.BlockSpec((nb, c, 4 * MIX_W), tok),
                  pl.BlockSpec((c, hw), lambda bi, i: (i, 0)),
                  pl.BlockSpec((c, hw), lambda bi, i: (i, 0))]
                 + [_const_spec(a.shape) for a in (decay, xi, zeta, gend, ng, eh)],
        out_specs=pl.BlockSpec((nb, c, MIX_W), tok),
        out_shape=jax.ShapeDtypeStruct((b, s, MIX_W), BF16),
        scratch_shapes=[pltpu.VMEM((nb, MIX_W, MIX_W), F32)],
        compiler_params=_params("arbitrary", "arbitrary"),
        name="retention",
    )(z3, cos, sin, decay, xi, zeta, gend, ng, eh)


def _to_time_major(src_ref, slab_ref, n_slabs):
    nb, sb, _ = src_ref.shape
    for bi in range(nb):
        for ks in range(n_slabs):
            slab_ref[ks, pl.ds(bi, sb, stride=nb), :] = src_ref[bi, :, ks * LANES:(ks + 1) * LANES]


def _from_time_major(slab_ref, dst_ref, n_slabs):
    nb, sb, _ = dst_ref.shape
    for bi in range(nb):
        cols = [slab_ref[ks, pl.ds(bi, sb, stride=nb), :] for ks in range(n_slabs)]
        dst_ref[bi, :, :] = jnp.concatenate(cols, axis=1).astype(dst_ref.dtype)


def _lru_body(z_ref, cw_ref, cb_ref, wbd_ref, bias_ref, c_ref, y_ref,
              slab_ref, a_ref, u_ref, tail_ref, h_ref):
    nb, sb, _ = z_ref.shape
    n = nb * sb
    w = MIX_W
    halo = (CONV_W - 1) * nb

    @pl.when(pl.program_id(0) == 0)
    def _():
        tail_ref[...] = jnp.zeros_like(tail_ref)
        h_ref[...] = jnp.zeros_like(h_ref)

    _to_time_major(z_ref, slab_ref, 2 * w // LANES)
    xg = jnp.concatenate([slab_ref[0], slab_ref[1]], axis=1)
    xi = jnp.concatenate([slab_ref[2], slab_ref[3]], axis=1)

    xpad = jnp.concatenate([tail_ref[...], xi], axis=0)
    xc = cb_ref[...]
    for j in range(CONV_W):
        xc = xc + cw_ref[j:j + 1, :] * xpad[j * nb:j * nb + n]
    tail_ref[...] = xi[n - halo:n]

    gates = jax.nn.sigmoid(_dot(xc.astype(BF16), wbd_ref[...]) + bias_ref[...])
    r = gates[:, 0:w]
    ig = gates[:, w:2 * w]
    a = jnp.exp(c_ref[...] * r)
    a_ref[...] = a
    u_ref[...] = jnp.sqrt(1.0 - a * a) * (ig * xc)

    def step(t, h):
        i = pl.multiple_of(t * nb, nb)
        h = a_ref[pl.ds(i, nb), :] * h + u_ref[pl.ds(i, nb), :]
        u_ref[pl.ds(i, nb), :] = h
        return h

    h_ref[...] = lax.fori_loop(0, sb, step, h_ref[...], unroll=8)

    out = u_ref[...] * _gelu_tanh(xg)
    slab_ref[0] = out[:, 0:LANES]
    slab_ref[1] = out[:, LANES:2 * LANES]
    _from_time_major(slab_ref, y_ref, w // LANES)


def _rglru(z3, cw, cb, wbd, bias, c_row):
    nb, s, _ = z3.shape
    sb = SCAN_BLOCK
    n = nb * sb
    assert nb == SUBLANES
    blk = lambda i: (0, i, 0)
    return pl.pallas_call(
        _lru_body,
        grid=(s // sb,),
        in_specs=[pl.BlockSpec((nb, sb, 2 * MIX_W), blk)]
                 + [_const_spec(a.shape) for a in (cw, cb, wbd, bias, c_row)],
        out_specs=pl.BlockSpec((nb, sb, MIX_W), blk),
        out_shape=jax.ShapeDtypeStruct((nb, s, MIX_W), BF16),
        scratch_shapes=[pltpu.VMEM((2 * MIX_W // LANES, n, LANES), F32),
                        pltpu.VMEM((n, MIX_W), F32),
                        pltpu.VMEM((n, MIX_W), F32),
                        pltpu.VMEM(((CONV_W - 1) * nb, MIX_W), F32),
                        pltpu.VMEM((nb, MIX_W), F32)],
        compiler_params=_params("arbitrary"),
        name="rglru",
    )(z3, cw, cb, wbd, bias, c_row)


def _s5_body(z_ref, lre_ref, lim_ref, bd_ref, cm_ref, d_ref, gw_ref, gb_ref, y_ref,
             slab_ref, bu_ref, hre_ref, him_ref):
    nb, sb, _ = z_ref.shape
    w = MIX_W
    ns = S5_LANES

    @pl.when(pl.program_id(0) == 0)
    def _():
        hre_ref[...] = jnp.zeros_like(hre_ref)
        him_ref[...] = jnp.zeros_like(him_ref)

    _to_time_major(z_ref, slab_ref, w // LANES)
    u = jnp.concatenate([slab_ref[0], slab_ref[1]], axis=1)
    u_bf = u.astype(BF16)
    half = (nb * sb) // 2
    bu_ref[0:half, :] = _dot(u_bf[0:half], bd_ref[...])
    bu_ref[half:2 * half, :] = _dot(u_bf[half:2 * half], bd_ref[...])

    lre = jnp.broadcast_to(lre_ref[...], (nb, ns))
    lim = jnp.broadcast_to(lim_ref[...], (nb, ns))

    def step(t, carry):
        hre, him = carry
        i = pl.multiple_of(t * nb, nb)
        nre = lre * hre - lim * him + bu_ref[pl.ds(i, nb), 0:ns]
        nim = lre * him + lim * hre + bu_ref[pl.ds(i, nb), ns:2 * ns]
        bu_ref[pl.ds(i, nb), 0:ns] = nre
        bu_ref[pl.ds(i, nb), ns:2 * ns] = nim
        return nre, nim

    hre, him = lax.fori_loop(0, sb, step, (hre_ref[...], him_ref[...]), unroll=2)
    hre_ref[...] = hre
    him_ref[...] = him

    halves = (slice(0, half), slice(half, 2 * half))
    ys = [_dot(bu_ref[rows, :].astype(BF16), cm_ref[...]) for rows in halves]
    acts = [_gelu_tanh(y + d_ref[...] * u[rows]) for y, rows in zip(ys, halves)]
    glus = [_dot(act.astype(BF16), gw_ref[...]) for act in acts]
    for rows, act, glu in zip(halves, acts, glus):
        out = act * jax.nn.sigmoid(glu + gb_ref[...])
        slab_ref[0, rows, :] = out[:, 0:LANES]
        slab_ref[1, rows, :] = out[:, LANES:2 * LANES]
    _from_time_major(slab_ref, y_ref, w // LANES)


def _s5(z3, lre, lim, bd, cm, d_row, gw, gb):
    nb, s, _ = z3.shape
    sb = SCAN_BLOCK
    n = nb * sb
    assert nb == SUBLANES
    blk = lambda i: (0, i, 0)
    return pl.pallas_call(
        _s5_body,
        grid=(s // sb,),
        in_specs=[pl.BlockSpec((nb, sb, MIX_W), blk)]
                 + [_const_spec(a.shape) for a in (lre, lim, bd, cm, d_row, gw, gb)],
        out_specs=pl.BlockSpec((nb, sb, MIX_W), blk),
        out_shape=jax.ShapeDtypeStruct((nb, s, MIX_W), BF16),
        scratch_shapes=[pltpu.VMEM((MIX_W // LANES, n, LANES), F32),
                        pltpu.VMEM((n, 2 * S5_LANES), F32),
                        pltpu.VMEM((nb, S5_LANES), F32),
                        pltpu.VMEM((nb, S5_LANES), F32)],
        compiler_params=_params("arbitrary"),
        name="s5",
    )(z3, lre, lim, bd, cm, d_row, gw, gb)


def _block_diag(blocks):
    g, r, c = blocks.shape
    eye = jnp.eye(g, dtype=blocks.dtype)
    return jnp.einsum('gh,grc->grhc', eye, blocks).reshape(g * r, g * c)


def _rotary_column_perm():
    e = jnp.arange(2)[:, None, None]
    h = jnp.arange(HEADS)[None, :, None]
    i = jnp.arange(HEAD_W // 2)[None, None, :]
    return (h * HEAD_W + 2 * i + e).reshape(-1)


def _prep_w_in(w_in):
    perm = _rotary_column_perm()
    base = 4 * MIX_W
    cols = jnp.arange(w_in.shape[1])
    cols = cols.at[base:base + MIX_W].set(base + perm)
    cols = cols.at[base + MIX_W:base + 2 * MIX_W].set(base + MIX_W + perm)
    return w_in[:, cols].astype(BF16)


def _prep_s5(lam_re, lam_im, b_re, b_im, c_re, c_im, log_dt):
    step = jnp.exp(log_dt)[:, None]
    mag = jnp.exp(lam_re * step)
    lb_re = mag * jnp.cos(lam_im * step)
    lb_im = mag * jnp.sin(lam_im * step)
    den = lam_re * lam_re + lam_im * lam_im
    f_re = ((lb_re - 1.0) * lam_re + lb_im * lam_im) / den
    f_im = (lb_im * lam_re - (lb_re - 1.0) * lam_im) / den
    bb_re = f_re[..., None] * b_re - f_im[..., None] * b_im
    bb_im = f_re[..., None] * b_im + f_im[..., None] * b_re
    bd = jnp.concatenate([_block_diag(bb_re.transpose(0, 2, 1)),
                          _block_diag(bb_im.transpose(0, 2, 1))], axis=1)
    cm = jnp.concatenate([_block_diag(c_re.transpose(0, 2, 1)),
                          -_block_diag(c_im.transpose(0, 2, 1))], axis=0)
    return (lb_re.reshape(1, -1), lb_im.reshape(1, -1), bd.astype(BF16), cm.astype(BF16))


def _row(v):
    return v.reshape(1, -1).astype(F32)


def kernel(x, mem, hg_lower_bounds, norm_mix_pre, norm_mix_post, w_in, w_gate, b_gate, hg_norm, ret_norm, lru_conv_w, lru_conv_b, lru_wa, lru_ba, lru_wx, lru_bx, lru_lambda, s5_lam_re, s5_lam_im, s5_b_re, s5_b_im, s5_c_re, s5_c_im, s5_d, s5_log_dt, s5_glu_w, s5_glu_b, w_up, w_out, norm_xa_pre, norm_xa_post, norm_mem, xa_w_q, xa_w_kv, xa_w_o, norm_ffn_pre, norm_ffn_post, ffn_w_gu, ffn_w_down):
    b, s, d = x.shape
    depth = w_in.shape[0]
    t = b * s
    p = jax.nn.softmax(hg_lower_bounds.astype(F32), axis=0)
    lower_bounds = jnp.cumsum(p, axis=0) - p[0:1]
    mem2 = mem.reshape(-1, d)

    x2 = x.reshape(t, d)
    for l in range(depth):
        z_hg, z_ret, z_lru, z_s5 = _in_proj(x2, _row(norm_mix_pre[l]), _prep_w_in(w_in[l]))
        y_a = _hgrn2(z_hg.reshape(b, s, -1), _row(lower_bounds[l]), _row(hg_norm[l]))
        y_b = _retention(z_ret.reshape(b, s, -1), _row(ret_norm[l]))
        wbd = jnp.concatenate([_block_diag(lru_wa[l]), _block_diag(lru_wx[l])], axis=1).astype(BF16)
        bias = jnp.concatenate([lru_ba[l], lru_bx[l]]).reshape(1, -1)
        c_row = _row(-LRU_C * jax.nn.softplus(-lru_lambda[l].astype(F32)))
        y_c = _rglru(z_lru.reshape(b, s, -1), lru_conv_w[l], _row(lru_conv_b[l]), wbd, bias, c_row)
        lre, lim, bd, cm = _prep_s5(s5_lam_re[l], s5_lam_im[l], s5_b_re[l], s5_b_im[l],
                                    s5_c_re[l], s5_c_im[l], s5_log_dt[l])
        y_d = _s5(z_s5.reshape(b, s, -1), lre, lim, bd, cm, _row(s5_d[l]),
                  s5_glu_w[l].astype(BF16), _row(s5_glu_b[l]))
        ys = [y.reshape(t, MIX_W) for y in (y_a, y_b, y_c, y_d)]
        x2 = _post(x2, ys, _row(norm_mix_pre[l]), _row(norm_mix_post[l]), w_gate[l].astype(BF16),
                   _row(b_gate[l]), w_up[l].astype(BF16), w_out[l].astype(BF16))

        k2, v2 = _kv_proj(mem2, _row(norm_mem[l]), xa_w_kv[l].astype(BF16))
        x3 = _xattn(x2.reshape(b, s, d), k2.reshape(b, -1, d), v2.reshape(b, -1, d),
                    _row(norm_xa_pre[l]), _row(norm_xa_post[l]),
                    xa_w_q[l].astype(BF16), xa_w_o[l].astype(BF16))
        x2 = _ffn(x3.reshape(t, d), _row(norm_ffn_pre[l]), _row(norm_ffn_post[l]),
                  ffn_w_gu[l].astype(BF16), ffn_w_down[l].astype(BF16))
    return x2.reshape(b, s, d)
```

```python
import functools
import math

import jax
import jax.numpy as jnp
from jax import lax
from jax.experimental import pallas as pl
from jax.experimental.pallas import tpu as pltpu

F32 = jnp.float32
BF16 = jnp.bfloat16

EPS = 1e-6
LANES = 128
SUBLANES = 8
MXU_DIM = 256
VMEM_LIMIT = 56 * 1024 * 1024

MIX_W = 256
HEADS = 4
HEAD_W = MIX_W // HEADS
HG_CHUNK = 32
SEQ_BLOCK = 256
HG_BATCH_GROUP = 2
RET_BATCH_GROUP = 4
ROPE_BASE = 10000.0
LRU_C = 8.0
CONV_W = 4
S5_GROUPS = 16
S5_P = 16
S5_N = 64
S5_LANES = S5_GROUPS * S5_N
ROW_TILE = 512
SCAN_BLOCK = 128


def _dot(a, b):
    return jnp.dot(a, b, preferred_element_type=F32)


def _dot_nt(a, b):
    return lax.dot_general(a, b, (((1,), (1,)), ((), ())), preferred_element_type=F32)


def _dot_tn(a, b):
    return lax.dot_general(a, b, (((0,), (0,)), ((), ())), preferred_element_type=F32)


def _rms(x, g):
    return x * lax.rsqrt(jnp.mean(x * x, axis=-1, keepdims=True) + EPS) * g


def _silu(x):
    return x * jax.nn.sigmoid(x)


def _gelu_tanh(x):
    return 0.5 * x * (1.0 + jnp.tanh(math.sqrt(2.0 / math.pi) * (x + 0.044715 * (x * x * x))))


def _split2(x):
    hi = x.astype(BF16)
    lo = (x - hi.astype(F32)).astype(BF16)
    return hi, lo


def _params(*sem):
    return pltpu.CompilerParams(dimension_semantics=sem, vmem_limit_bytes=VMEM_LIMIT)


def _spec(p):
    if isinstance(p, tuple):
        arr, layer = p
        shape = arr.shape[1:]
        return pl.BlockSpec((None,) + shape, lambda *_: (layer,) + (0,) * len(shape))
    return pl.BlockSpec(p.shape, lambda *_: (0,) * p.ndim)


def _arr(p):
    return p[0] if isinstance(p, tuple) else p


def _in_proj_body(x_ref, g_ref, w_ref, wrot_ref, hg_ref, ret_ref, lru_ref, s5_ref):
    w = MIX_W
    h = _rms(x_ref[...], g_ref[...]).astype(BF16)
    hg_ref[...] = _dot(h, w_ref[:, 0:4 * w])
    ret_ref[:, 0:2 * w] = _dot(h, wrot_ref[...])
    ret_ref[:, 2 * w:4 * w] = _dot(h, w_ref[:, 6 * w:8 * w])
    lru_ref[...] = _dot(h, w_ref[:, 8 * w:10 * w])
    s5_ref[...] = _dot(h, w_ref[:, 10 * w:11 * w])


def _in_proj(x2, g, w, wrot):
    t, d = x2.shape
    widths = (4 * MIX_W, 4 * MIX_W, 2 * MIX_W, MIX_W)
    return pl.pallas_call(
        _in_proj_body,
        grid=(t // ROW_TILE,),
        in_specs=[pl.BlockSpec((ROW_TILE, d), lambda i: (i, 0)), _spec(g), _spec(w), _spec(wrot)],
        out_specs=[pl.BlockSpec((ROW_TILE, wd), lambda i: (i, 0)) for wd in widths],
        out_shape=[jax.ShapeDtypeStruct((t, wd), F32) for wd in widths],
        compiler_params=_params("arbitrary"),
        name="in_proj",
    )(x2, _arr(g), _arr(w), _arr(wrot))


def _post_body(x_ref, y0_ref, y1_ref, y2_ref, y3_ref, gpre_ref, gpost_ref, wg_ref, bg_ref,
               wup_ref, wout_ref, o_ref):
    x = x_ref[...]
    d = x.shape[-1]
    ups = [_dot(y_ref[...], wup_ref[n]) for n, y_ref in enumerate((y0_ref, y1_ref, y2_ref, y3_ref))]
    h = _rms(x, gpre_ref[...]).astype(BF16)
    merged = None
    for n, up in enumerate(ups):
        gate = jax.nn.sigmoid(_dot(h, wg_ref[:, n * d:(n + 1) * d]) + bg_ref[:, n * d:(n + 1) * d])
        term = gate * up
        merged = term if merged is None else merged + term
    out = _dot(merged.astype(BF16), wout_ref[...])
    o_ref[...] = x + _rms(out, gpost_ref[...])


def _post(x2, ys, gpre, gpost, wg, bg, wup, wout):
    t, d = x2.shape
    row = lambda i: (i, 0)
    return pl.pallas_call(
        _post_body,
        grid=(t // ROW_TILE,),
        in_specs=[pl.BlockSpec((ROW_TILE, d), row)]
                 + [pl.BlockSpec((ROW_TILE, MIX_W), row) for _ in ys]
                 + [_spec(a) for a in (gpre, gpost, wg, bg, wup, wout)],
        out_specs=pl.BlockSpec((ROW_TILE, d), row),
        out_shape=jax.ShapeDtypeStruct((t, d), F32),
        compiler_params=_params("arbitrary"),
        name="mix_post",
    )(x2, *ys, *[_arr(a) for a in (gpre, gpost, wg, bg, wup, wout)])


def _kv_body(m_ref, g_ref, w_ref, k_ref, v_ref):
    d = m_ref.shape[-1]
    m = _rms(m_ref[...], g_ref[...]).astype(BF16)
    k_ref[...] = _dot(m, w_ref[:, 0:d]).astype(BF16)
    v_ref[...] = _dot(m, w_ref[:, d:2 * d]).astype(BF16)


def _kv_proj(mem2, g, w):
    t, d = mem2.shape
    tile = min(ROW_TILE, t)
    row = lambda i: (i, 0)
    return pl.pallas_call(
        _kv_body,
        grid=(t // tile,),
        in_specs=[pl.BlockSpec((tile, d), row), _spec(g), _spec(w)],
        out_specs=[pl.BlockSpec((tile, d), row), pl.BlockSpec((tile, d), row)],
        out_shape=[jax.ShapeDtypeStruct((t, d), BF16), jax.ShapeDtypeStruct((t, d), BF16)],
        compiler_params=_params("arbitrary"),
        name="kv_proj",
    )(mem2, _arr(g), _arr(w))


def _xattn_body(x_ref, k_ref, v_ref, gpre_ref, gpost_ref, wq_ref, wo_ref, o_ref):
    x = x_ref[...]
    d = x.shape[-1]
    dh = d // HEADS
    heads = [slice(hd * dh, (hd + 1) * dh) for hd in range(HEADS)]
    h = _rms(x, gpre_ref[...]).astype(BF16)
    q = _dot(h, wq_ref[...])
    scores = [_dot_nt(q[:, sl].astype(BF16), k_ref[:, sl]) * (dh ** -0.5) for sl in heads]
    probs = []
    for s in scores:
        p = jnp.exp(s - jnp.max(s, axis=-1, keepdims=True))
        probs.append((p / jnp.sum(p, axis=-1, keepdims=True)).astype(BF16))
    outs = [_dot(p, v_ref[:, sl]) for p, sl in zip(probs, heads)]
    o = jnp.concatenate(outs, axis=1).astype(BF16)
    o_ref[...] = x + _rms(_dot(o, wo_ref[...]), gpost_ref[...])


def _xattn(x2, k3, v3, gpre, gpost, wq, wo):
    t, d = x2.shape
    b, m, _ = k3.shape
    per_batch = t // ROW_TILE // b
    row = lambda i: (i, 0)
    mem = lambda i: (i // per_batch, 0, 0)
    return pl.pallas_call(
        _xattn_body,
        grid=(t // ROW_TILE,),
        in_specs=[pl.BlockSpec((ROW_TILE, d), row),
                  pl.BlockSpec((None, m, d), mem), pl.BlockSpec((None, m, d), mem)]
                 + [_spec(a) for a in (gpre, gpost, wq, wo)],
        out_specs=pl.BlockSpec((ROW_TILE, d), row),
        out_shape=jax.ShapeDtypeStruct((t, d), F32),
        compiler_params=_params("arbitrary"),
        name="xattn",
    )(x2, k3, v3, *[_arr(a) for a in (gpre, gpost, wq, wo)])


def _ffn_body(ff_edges, x_ref, gpre_ref, gpost_ref, wgu_ref, wdn_ref, o_ref):
    x = x_ref[...]
    d_ff = wdn_ref.shape[0]
    h = _rms(x, gpre_ref[...]).astype(BF16)
    acc = None
    for c0, c1 in zip(ff_edges[:-1], ff_edges[1:]):
        gate = _dot(h, wgu_ref[:, c0:c1])
        up = _dot(h, wgu_ref[:, d_ff + c0:d_ff + c1])
        part = _dot((_silu(gate) * up).astype(BF16), wdn_ref[c0:c1, :])
        acc = part if acc is None else acc + part
    o_ref[...] = x + _rms(acc, gpost_ref[...])


def _ffn(x2, gpre, gpost, wgu, wdn):
    t, d = x2.shape
    d_ff = _arr(wdn).shape[-2]
    n_tiles = d_ff // MXU_DIM
    assert n_tiles * MXU_DIM == d_ff
    ff_edges = (0, (n_tiles + 1) // 2 * MXU_DIM, d_ff)
    row = lambda i: (i, 0)
    return pl.pallas_call(
        functools.partial(_ffn_body, ff_edges),
        grid=(t // ROW_TILE,),
        in_specs=[pl.BlockSpec((ROW_TILE, d), row)]
                 + [_spec(a) for a in (gpre, gpost, wgu, wdn)],
        out_specs=pl.BlockSpec((ROW_TILE, d), row),
        out_shape=jax.ShapeDtypeStruct((t, d), F32),
        compiler_params=_params("arbitrary"),
        name="ffn",
    )(x2, *[_arr(a) for a in (gpre, gpost, wgu, wdn)])


def _head_norm_gate(o, zg, ng, e_head):
    ms = _dot((o * o).astype(BF16), e_head)
    return o * lax.rsqrt(ms + EPS) * ng * _silu(zg)


def _hgrn2_body(z_ref, lb_ref, ng_ref, le_ref, eh_ref, y_ref, st_ref):
    @pl.when(pl.program_id(1) == 0)
    def _():
        st_ref[...] = jnp.zeros_like(st_ref)

    for bi in range(z_ref.shape[0]):
        _hgrn2_one(z_ref.at[bi], lb_ref, ng_ref, le_ref, eh_ref, y_ref.at[bi], st_ref.at[bi])


def _hgrn2_one(z_ref, lb_ref, ng_ref, le_ref, eh_ref, y_ref, st_ref):
    c = SEQ_BLOCK
    w = MIX_W
    n_chunks = c // HG_CHUNK

    zq = z_ref[:, 0:w]
    zf = z_ref[:, w:2 * w]
    v = z_ref[:, 2 * w:3 * w]
    zg = z_ref[:, 3 * w:4 * w]
    lb = lb_ref[...]
    f = lb + (1.0 - lb) * jax.nn.sigmoid(zf)
    logf = jnp.log(f)
    k = 1.0 - f
    q = _silu(zq)

    le = le_ref[...]
    hi, lo = _split2(logf)
    b = _dot(le, hi) + _dot(le, lo)
    b_last = b.reshape(n_chunks, HG_CHUNK, w)[:, HG_CHUNK - 1:HG_CHUNK, :]
    b_end = jnp.broadcast_to(b_last, (n_chunks, HG_CHUNK, w)).reshape(c, w)
    q_dec = q * jnp.exp(b)
    k_inv = (k * jnp.exp(-b)).astype(BF16)
    k_end = (k * jnp.exp(b_end - b)).astype(BF16)

    lane = lax.broadcasted_iota(jnp.int32, (1, w), 1)
    row = lax.broadcasted_iota(jnp.int32, (c, c), 0)
    col = lax.broadcasted_iota(jnp.int32, (c, c), 1)
    shift = HG_CHUNK.bit_length() - 1
    causal = (row >= col) & ((row >> shift) == (col >> shift))
    hshift = HEAD_W.bit_length() - 1

    o = None
    for hd in range(HEADS):
        mh = (lane >> hshift) == hd
        qh = jnp.where(mh, q_dec, 0.0).astype(BF16)
        a = jnp.where(causal, _dot_nt(qh, k_inv), 0.0).astype(BF16)
        term = _dot(a, jnp.where(mh, v, 0.0).astype(BF16))
        o = term if o is None else o + term

    st = st_ref[...]
    head_bd = (row >> hshift) == (col >> hshift)
    q_bf = q_dec.astype(BF16)
    vt_bf = v.T.astype(BF16)
    kvs = []
    for n in range(n_chunks):
        sl = slice(n * HG_CHUNK, (n + 1) * HG_CHUNK)
        pieces = [k_end[sl]]
        if n > 0:
            pieces.insert(0, jnp.zeros((n * HG_CHUNK, w), BF16))
        if (n + 1) * HG_CHUNK < c:
            pieces.append(jnp.zeros((c - (n + 1) * HG_CHUNK, w), BF16))
        kvs.append(_dot(vt_bf, jnp.concatenate(pieces, axis=0)))
    states = []
    for n in range(n_chunks):
        states.append(st.astype(BF16))
        dec = jnp.exp(b_end[n * HG_CHUNK:n * HG_CHUNK + 1, :])
        st = st * dec + jnp.where(head_bd, kvs[n], 0.0)
    st_ref[...] = st
    parts = []
    for n in range(n_chunks):
        sl = slice(n * HG_CHUNK, (n + 1) * HG_CHUNK)
        parts.append(o[sl] + _dot_nt(q_bf[sl], states[n]))
    o = jnp.concatenate(parts, axis=0)

    y_ref[...] = _head_norm_gate(o, zg, ng_ref[...], eh_ref[...]).astype(BF16)


def _chunk_sum_matrix(c):
    r = jnp.arange(c)
    same = (r[:, None] // HG_CHUNK) == (r[None, :] // HG_CHUNK)
    lower = same & (r[:, None] >= r[None, :])
    return lower.astype(BF16)


def _head_mean_matrix():
    r = jnp.arange(MIX_W)
    same = (r[:, None] // HEAD_W) == (r[None, :] // HEAD_W)
    return (same.astype(F32) / HEAD_W).astype(BF16)


def _hgrn2(z3, lb, ng):
    b, s, _ = z3.shape
    c = SEQ_BLOCK
    le = _chunk_sum_matrix(c)
    eh = _head_mean_matrix()
    nb = HG_BATCH_GROUP
    tok = lambda bi, i: (bi, i, 0)
    return pl.pallas_call(
        _hgrn2_body,
        grid=(b // nb, s // c),
        in_specs=[pl.BlockSpec((nb, c, 4 * MIX_W), tok)]
                 + [_spec(a) for a in (lb, ng, le, eh)],
        out_specs=pl.BlockSpec((nb, c, MIX_W), tok),
        out_shape=jax.ShapeDtypeStruct((b, s, MIX_W), BF16),
        scratch_shapes=[pltpu.VMEM((nb, MIX_W, MIX_W), F32)],
        compiler_params=_params("arbitrary", "arbitrary"),
        name="hgrn2",
    )(z3, *[_arr(a) for a in (lb, ng, le, eh)])


def _ret_body(z_ref, cos_ref, sin_ref, dec_ref, xi_ref, zeta_ref, gend_ref, ng_ref, eh_ref,
              y_ref, st_ref):
    @pl.when(pl.program_id(1) == 0)
    def _():
        st_ref[...] = jnp.zeros_like(st_ref)

    for bi in range(z_ref.shape[0]):
        _ret_one(z_ref.at[bi], cos_ref, sin_ref, dec_ref, xi_ref, zeta_ref, gend_ref, ng_ref, eh_ref,
                 y_ref.at[bi], st_ref.at[bi])


def _ret_one(z_ref, cos_ref, sin_ref, dec_ref, xi_ref, zeta_ref, gend_ref, ng_ref, eh_ref,
             y_ref, st_ref):
    w = MIX_W
    hw = w // 2

    cs = cos_ref[...]
    sn = sin_ref[...]
    q1 = z_ref[:, 0:hw]
    q2 = z_ref[:, hw:w]
    k1 = z_ref[:, w:w + hw]
    k2 = z_ref[:, w + hw:2 * w]
    v = z_ref[:, 2 * w:3 * w]
    zg = z_ref[:, 3 * w:4 * w]
    qr = jnp.concatenate([q1 * cs - q2 * sn, q1 * sn + q2 * cs], axis=1)
    kr = jnp.concatenate([k1 * cs - k2 * sn, k1 * sn + k2 * cs], axis=1) * (HEAD_W ** -0.5)
    kr_bf = kr.astype(BF16)

    lane = lax.broadcasted_iota(jnp.int32, (1, w), 1)
    rshift = (hw // HEADS).bit_length() - 1
    hshift = HEAD_W.bit_length() - 1
    head_rot = (lane & (hw - 1)) >> rshift
    head_nat = lane >> hshift

    o = None
    for hd in range(HEADS):
        qh = jnp.where(head_rot == hd, qr, 0.0).astype(BF16)
        sc = (_dot_nt(qh, kr_bf) * dec_ref[hd]).astype(BF16)
        term = _dot(sc, jnp.where(head_nat == hd, v, 0.0).astype(BF16))
        o = term if o is None else o + term

    st = st_ref[...]
    o = o + _dot_nt((qr * xi_ref[...]).astype(BF16), st.astype(BF16))
    kv = _dot_tn(v.astype(BF16), (kr * zeta_ref[...]).astype(BF16))
    row = lax.broadcasted_iota(jnp.int32, (w, w), 0)
    col = lax.broadcasted_iota(jnp.int32, (w, w), 1)
    head_bd = (row >> hshift) == ((col & (hw - 1)) >> rshift)
    st_ref[...] = st * gend_ref[...] + jnp.where(head_bd, kv, 0.0)

    y_ref[...] = _head_norm_gate(o, zg, ng_ref[...], eh_ref[...]).astype(BF16)


def _rot_lane_head():
    lane = jnp.arange(MIX_W)
    return (lane % (MIX_W // 2)) // (MIX_W // 2 // HEADS)


def _retention(z3, ng):
    b, s, _ = z3.shape
    c = SEQ_BLOCK
    hw = MIX_W // 2
    n_freq = HEAD_W // 2
    pos = jnp.arange(s, dtype=F32)
    inv_freq = ROPE_BASE ** (-jnp.arange(0, HEAD_W, 2, dtype=F32) / HEAD_W)
    ang = pos[:, None] * jnp.tile(inv_freq, hw // n_freq)[None, :]
    cos, sin = jnp.cos(ang), jnp.sin(ang)
    log_gamma = jnp.log1p(-jnp.power(2.0, -5.0 - jnp.arange(HEADS, dtype=F32)))
    idx = jnp.arange(c, dtype=F32)
    rel = idx[:, None] - idx[None, :]
    causal = rel >= 0
    decay = jnp.where(causal, jnp.exp(jnp.where(causal, rel, 0.0)[None] * log_gamma[:, None, None]), 0.0)
    lg_lane = log_gamma[_rot_lane_head()]
    xi = jnp.exp((idx + 1.0)[:, None] * lg_lane[None, :])
    zeta = jnp.exp((c - 1.0 - idx)[:, None] * lg_lane[None, :])
    gend = jnp.exp(c * lg_lane)[None, :]
    eh = _head_mean_matrix()
    nb = RET_BATCH_GROUP
    tok = lambda bi, i: (bi, i, 0)
    return pl.pallas_call(
        _ret_body,
        grid=(b // nb, s // c),
        in_specs=[pl.BlockSpec((nb, c, 4 * MIX_W), tok),
                  pl.BlockSpec((c, hw), lambda bi, i: (i, 0)),
                  pl.BlockSpec((c, hw), lambda bi, i: (i, 0))]
                 + [_spec(a) for a in (decay, xi, zeta, gend, ng, eh)],
        out_specs=pl.BlockSpec((nb, c, MIX_W), tok),
        out_shape=jax.ShapeDtypeStruct((b, s, MIX_W), BF16),
        scratch_shapes=[pltpu.VMEM((nb, MIX_W, MIX_W), F32)],
        compiler_params=_params("arbitrary", "arbitrary"),
        name="retention",
    )(z3, cos, sin, *[_arr(a) for a in (decay, xi, zeta, gend, ng, eh)])


def _to_time_major(src_ref, slab_ref, n_slabs):
    nb, sb, _ = src_ref.shape
    for bi in range(nb):
        for ks in range(n_slabs):
            slab_ref[ks, pl.ds(bi, sb, stride=nb), :] = src_ref[bi, :, ks * LANES:(ks + 1) * LANES]


def _from_time_major(slab_ref, dst_ref, n_slabs):
    nb, sb, _ = dst_ref.shape
    for bi in range(nb):
        cols = [slab_ref[ks, pl.ds(bi, sb, stride=nb), :] for ks in range(n_slabs)]
        dst_ref[bi, :, :] = jnp.concatenate(cols, axis=1).astype(dst_ref.dtype)


def _scan_body(zl_ref, zs_ref, cw_ref, cb_ref, wbd_ref, bias_ref, c_ref,
               lre_ref, lim_ref, bd_ref, cm_ref, d_ref, gw_ref, gb_ref, yl_ref, ys_ref,
               lslab_ref, sslab_ref, a_ref, u_ref, tail_ref, hl_ref, bu_ref, hre_ref, him_ref):
    nb, sb, _ = zl_ref.shape
    n = nb * sb
    half = n // 2
    halves = (slice(0, half), slice(half, n))
    w = MIX_W
    ns = S5_LANES
    halo = (CONV_W - 1) * nb

    @pl.when(pl.program_id(0) == 0)
    def _():
        tail_ref[...] = jnp.zeros_like(tail_ref)
        hl_ref[...] = jnp.zeros_like(hl_ref)
        hre_ref[...] = jnp.zeros_like(hre_ref)
        him_ref[...] = jnp.zeros_like(him_ref)

    _to_time_major(zs_ref, sslab_ref, w // LANES)
    _to_time_major(zl_ref, lslab_ref, 2 * w // LANES)

    su = jnp.concatenate([sslab_ref[0], sslab_ref[1]], axis=1)
    su_bf = su.astype(BF16)
    xg = jnp.concatenate([lslab_ref[0], lslab_ref[1]], axis=1)
    xi = jnp.concatenate([lslab_ref[2], lslab_ref[3]], axis=1)
    xpad = jnp.concatenate([tail_ref[...], xi], axis=0)
    tail_ref[...] = xi[n - halo:n]
    for rows in halves:
        bu_ref[rows, :] = _dot(su_bf[rows], bd_ref[...])
        xc = cb_ref[...]
        for j in range(CONV_W):
            xc = xc + cw_ref[j:j + 1, :] * xpad[rows.start + j * nb:rows.stop + j * nb]
        gates = jax.nn.sigmoid(_dot(xc.astype(BF16), wbd_ref[...]) + bias_ref[...])
        r = gates[:, 0:w]
        ig = gates[:, w:2 * w]
        a = jnp.exp(c_ref[...] * r)
        a_ref[rows, :] = a
        u_ref[rows, :] = jnp.sqrt(1.0 - a * a) * (ig * xc)

    lre = jnp.broadcast_to(lre_ref[...], (nb, ns))
    lim = jnp.broadcast_to(lim_ref[...], (nb, ns))

    def step(t, carry):
        hl, hre, him = carry
        i = pl.multiple_of(t * nb, nb)
        nre = lre * hre - lim * him + bu_ref[pl.ds(i, nb), 0:ns]
        nim = lre * him + lim * hre + bu_ref[pl.ds(i, nb), ns:2 * ns]
        bu_ref[pl.ds(i, nb), 0:ns] = nre
        bu_ref[pl.ds(i, nb), ns:2 * ns] = nim
        hl = a_ref[pl.ds(i, nb), :] * hl + u_ref[pl.ds(i, nb), :]
        u_ref[pl.ds(i, nb), :] = hl
        return hl, nre, nim

    hl, hre, him = lax.fori_loop(0, sb, step, (hl_ref[...], hre_ref[...], him_ref[...]), unroll=2)
    hl_ref[...] = hl
    hre_ref[...] = hre
    him_ref[...] = him

    ys = []
    for rows in halves:
        ys.append(_dot(bu_ref[rows, :].astype(BF16), cm_ref[...]))
        lru_out = u_ref[rows, :] * _gelu_tanh(xg[rows])
        lslab_ref[0, rows, :] = lru_out[:, 0:LANES]
        lslab_ref[1, rows, :] = lru_out[:, LANES:2 * LANES]
    acts = [_gelu_tanh(y + d_ref[...] * su[rows]) for y, rows in zip(ys, halves)]
    glus = [_dot(act.astype(BF16), gw_ref[...]) for act in acts]
    for rows, act, glu in zip(halves, acts, glus):
        out = act * jax.nn.sigmoid(glu + gb_ref[...])
        sslab_ref[0, rows, :] = out[:, 0:LANES]
        sslab_ref[1, rows, :] = out[:, LANES:2 * LANES]
    _from_time_major(lslab_ref, yl_ref, w // LANES)
    _from_time_major(sslab_ref, ys_ref, w // LANES)


def _scan_mixers(zl3, zs3, lru_consts, s5_consts):
    nb, s, _ = zl3.shape
    sb = SCAN_BLOCK
    n = nb * sb
    assert nb == SUBLANES
    blk = lambda i: (0, i, 0)
    consts = tuple(lru_consts) + tuple(s5_consts)
    y_spec = pl.BlockSpec((nb, sb, MIX_W), blk)
    y_shape = jax.ShapeDtypeStruct((nb, s, MIX_W), BF16)
    return pl.pallas_call(
        _scan_body,
        grid=(s // sb,),
        in_specs=[pl.BlockSpec((nb, sb, 2 * MIX_W), blk), pl.BlockSpec((nb, sb, MIX_W), blk)]
                 + [_spec(a) for a in consts],
        out_specs=[y_spec, y_spec],
        out_shape=[y_shape, y_shape],
        scratch_shapes=[pltpu.VMEM((2 * MIX_W // LANES, n, LANES), F32),
                        pltpu.VMEM((MIX_W // LANES, n, LANES), F32),
                        pltpu.VMEM((n, MIX_W), F32),
                        pltpu.VMEM((n, MIX_W), F32),
                        pltpu.VMEM(((CONV_W - 1) * nb, MIX_W), F32),
                        pltpu.VMEM((nb, MIX_W), F32),
                        pltpu.VMEM((n, 2 * S5_LANES), F32),
                        pltpu.VMEM((nb, S5_LANES), F32),
                        pltpu.VMEM((nb, S5_LANES), F32)],
        compiler_params=_params("arbitrary"),
        name="scan_mixers",
    )(zl3, zs3, *[_arr(a) for a in consts])


def _block_diag(blocks):
    g, r, c = blocks.shape[-3:]
    eye = jnp.eye(g, dtype=blocks.dtype)
    out = jnp.einsum('gh,...grc->...grhc', eye, blocks)
    return out.reshape(blocks.shape[:-3] + (g * r, g * c))


def _rotary_column_perm():
    e = jnp.arange(2)[:, None, None]
    h = jnp.arange(HEADS)[None, :, None]
    i = jnp.arange(HEAD_W // 2)[None, None, :]
    return (h * HEAD_W + 2 * i + e).reshape(-1)


def _prep_w_rot(w_in):
    perm = _rotary_column_perm()
    base = 4 * MIX_W
    cols = jnp.concatenate([base + perm, base + MIX_W + perm])
    return w_in[..., cols].astype(BF16)


def _prep_s5(lam_re, lam_im, b_re, b_im, c_re, c_im, log_dt):
    step = jnp.exp(log_dt)[..., None]
    mag = jnp.exp(lam_re * step)
    lb_re = mag * jnp.cos(lam_im * step)
    lb_im = mag * jnp.sin(lam_im * step)
    den = lam_re * lam_re + lam_im * lam_im
    f_re = ((lb_re - 1.0) * lam_re + lb_im * lam_im) / den
    f_im = (lb_im * lam_re - (lb_re - 1.0) * lam_im) / den
    bb_re = f_re[..., None] * b_re - f_im[..., None] * b_im
    bb_im = f_re[..., None] * b_im + f_im[..., None] * b_re
    bd = jnp.concatenate([_block_diag(jnp.swapaxes(bb_re, -1, -2)),
                          _block_diag(jnp.swapaxes(bb_im, -1, -2))], axis=-1)
    cm = jnp.concatenate([_block_diag(jnp.swapaxes(c_re, -1, -2)),
                          -_block_diag(jnp.swapaxes(c_im, -1, -2))], axis=-2)
    n_layers = lam_re.shape[0]
    return (lb_re.reshape(n_layers, 1, -1), lb_im.reshape(n_layers, 1, -1),
            bd.astype(BF16), cm.astype(BF16))


def _rows(v):
    return v.reshape(v.shape[0], 1, -1).astype(F32)


def kernel(x, mem, hg_lower_bounds, norm_mix_pre, norm_mix_post, w_in, w_gate, b_gate, hg_norm, ret_norm, lru_conv_w, lru_conv_b, lru_wa, lru_ba, lru_wx, lru_bx, lru_lambda, s5_lam_re, s5_lam_im, s5_b_re, s5_b_im, s5_c_re, s5_c_im, s5_d, s5_log_dt, s5_glu_w, s5_glu_b, w_up, w_out, norm_xa_pre, norm_xa_post, norm_mem, xa_w_q, xa_w_kv, xa_w_o, norm_ffn_pre, norm_ffn_post, ffn_w_gu, ffn_w_down):
    b, s, d = x.shape
    depth = w_in.shape[0]
    t = b * s

    p = jax.nn.softmax(hg_lower_bounds.astype(F32), axis=0)
    lower_bounds = _rows(jnp.cumsum(p, axis=0) - p[0:1])
    w_in_bf, w_rot = w_in.astype(BF16), _prep_w_rot(w_in)
    w_gate_bf, w_up_bf, w_out_bf = w_gate.astype(BF16), w_up.astype(BF16), w_out.astype(BF16)
    w_q_bf, w_kv_bf, w_o_bf = xa_w_q.astype(BF16), xa_w_kv.astype(BF16), xa_w_o.astype(BF16)
    w_gu_bf, w_down_bf = ffn_w_gu.astype(BF16), ffn_w_down.astype(BF16)
    glu_w_bf = s5_glu_w.astype(BF16)
    lru_wbd = jnp.concatenate([_block_diag(lru_wa), _block_diag(lru_wx)], axis=-1).astype(BF16)
    lru_bias = _rows(jnp.concatenate([lru_ba, lru_bx], axis=-1))
    lru_c = _rows(-LRU_C * jax.nn.softplus(-lru_lambda.astype(F32)))
    lre, lim, bd, cm = _prep_s5(s5_lam_re, s5_lam_im, s5_b_re, s5_b_im, s5_c_re, s5_c_im, s5_log_dt)
    rows = {name: _rows(v) for name, v in dict(
        mix_pre=norm_mix_pre, mix_post=norm_mix_post, b_gate=b_gate, hg_norm=hg_norm,
        ret_norm=ret_norm, conv_b=lru_conv_b, s5_d=s5_d, glu_b=s5_glu_b, xa_pre=norm_xa_pre,
        xa_post=norm_xa_post, mem=norm_mem, ffn_pre=norm_ffn_pre, ffn_post=norm_ffn_post).items()}
    conv_w = lru_conv_w.astype(F32)
    mem2 = mem.reshape(-1, d)

    x2 = x.reshape(t, d)
    for l in range(depth):
        at = lambda stacked: (stacked, l)
        z_hg, z_ret, z_lru, z_s5 = _in_proj(x2, at(rows["mix_pre"]), at(w_in_bf), at(w_rot))
        y_a = _hgrn2(z_hg.reshape(b, s, -1), at(lower_bounds), at(rows["hg_norm"]))
        y_b = _retention(z_ret.reshape(b, s, -1), at(rows["ret_norm"]))
        y_c, y_d = _scan_mixers(
            z_lru.reshape(b, s, -1), z_s5.reshape(b, s, -1),
            (at(conv_w), at(rows["conv_b"]), at(lru_wbd), at(lru_bias), at(lru_c)),
            (at(lre), at(lim), at(bd), at(cm), at(rows["s5_d"]), at(glu_w_bf), at(rows["glu_b"])))
        ys = [y.reshape(t, MIX_W) for y in (y_a, y_b, y_c, y_d)]
        x2 = _post(x2, ys, at(rows["mix_pre"]), at(rows["mix_post"]), at(w_gate_bf),
                   at(rows["b_gate"]), at(w_up_bf), at(w_out_bf))

        k2, v2 = _kv_proj(mem2, at(rows["mem"]), at(w_kv_bf))
        x2 = _xattn(x2, k2.reshape(b, -1, d), v2.reshape(b, -1, d),
                    at(rows["xa_pre"]), at(rows["xa_post"]), at(w_q_bf), at(w_o_bf))
        x2 = _ffn(x2, at(rows["ffn_pre"]), at(rows["ffn_post"]), at(w_gu_bf), at(w_down_bf))
    return x2.reshape(b, s, d)
```

```python
import functools
import math

import jax
import jax.numpy as jnp
from jax import lax
from jax.experimental import pallas as pl
from jax.experimental.pallas import tpu as pltpu

F32 = jnp.float32
BF16 = jnp.bfloat16

EPS = 1e-6
LANES = 128
SUBLANES = 8
MXU_DIM = 256
VMEM_LIMIT = 56 * 1024 * 1024

MIX_W = 256
HEADS = 4
HEAD_W = MIX_W // HEADS
HG_CHUNK = 32
SEQ_BLOCK = 256
HG_BATCH_GROUP = 2
RET_BATCH_GROUP = 4
ROPE_BASE = 10000.0
LRU_C = 8.0
CONV_W = 4
S5_GROUPS = 16
S5_P = 16
S5_N = 64
S5_LANES = S5_GROUPS * S5_N
ROW_TILE = 512
IN_TILE = 1024
POST_TILE = 1024
XATTN_TILE = 1024
FFN_TILE = 1024
FFN_CHUNKS = 3
SCAN_BLOCK = 128


def _dot(a, b):
    return jnp.dot(a, b, preferred_element_type=F32)


def _dot_nt(a, b):
    return lax.dot_general(a, b, (((1,), (1,)), ((), ())), preferred_element_type=F32)


def _dot_tn(a, b):
    return lax.dot_general(a, b, (((0,), (0,)), ((), ())), preferred_element_type=F32)


def _rms(x, g):
    return x * lax.rsqrt(jnp.mean(x * x, axis=-1, keepdims=True) + EPS) * g


def _silu(x):
    return x * jax.nn.sigmoid(x)


def _gelu_tanh(x):
    return 0.5 * x * (1.0 + jnp.tanh(math.sqrt(2.0 / math.pi) * (x + 0.044715 * (x * x * x))))


def _split2(x):
    hi = x.astype(BF16)
    lo = (x - hi.astype(F32)).astype(BF16)
    return hi, lo


def _params(*sem):
    return pltpu.CompilerParams(dimension_semantics=sem, vmem_limit_bytes=VMEM_LIMIT)


def _spec(p):
    if isinstance(p, tuple):
        arr, layer = p
        shape = arr.shape[1:]
        return pl.BlockSpec((None,) + shape, lambda *_: (layer,) + (0,) * len(shape),
                            pipeline_mode=pl.Buffered(1))
    return pl.BlockSpec(p.shape, lambda *_: (0,) * p.ndim, pipeline_mode=pl.Buffered(1))


def _arr(p):
    return p[0] if isinstance(p, tuple) else p


def _in_proj_body(x_ref, g_ref, w_ref, wrot_ref, hg_ref, ret_ref, lru_ref, s5_ref):
    w = MIX_W
    h = _rms(x_ref[...], g_ref[...]).astype(BF16)
    hg_ref[...] = _dot(h, w_ref[:, 0:4 * w])
    ret_ref[:, 0:2 * w] = _dot(h, wrot_ref[...])
    ret_ref[:, 2 * w:4 * w] = _dot(h, w_ref[:, 6 * w:8 * w])
    lru_ref[...] = _dot(h, w_ref[:, 8 * w:10 * w])
    s5_ref[...] = _dot(h, w_ref[:, 10 * w:11 * w])


def _in_proj(x2, g, w, wrot):
    t, d = x2.shape
    widths = (4 * MIX_W, 4 * MIX_W, 2 * MIX_W, MIX_W)
    return pl.pallas_call(
        _in_proj_body,
        grid=(t // IN_TILE,),
        in_specs=[pl.BlockSpec((IN_TILE, d), lambda i: (i, 0)), _spec(g), _spec(w), _spec(wrot)],
        out_specs=[pl.BlockSpec((IN_TILE, wd), lambda i: (i, 0)) for wd in widths],
        out_shape=[jax.ShapeDtypeStruct((t, wd), F32) for wd in widths],
        compiler_params=_params("arbitrary"),
        name="in_proj",
    )(x2, _arr(g), _arr(w), _arr(wrot))


def _post_body(x_ref, y0_ref, y1_ref, y2_ref, y3_ref, gpre_ref, gpost_ref, wg_ref, bg_ref,
               wup_ref, wout_ref, o_ref):
    x = x_ref[...]
    d = x.shape[-1]
    h = _rms(x, gpre_ref[...]).astype(BF16)
    merged = None
    for n, y_ref in enumerate((y0_ref, y1_ref, y2_ref, y3_ref)):
        gate = jax.nn.sigmoid(_dot(h, wg_ref[:, n * d:(n + 1) * d]) + bg_ref[:, n * d:(n + 1) * d])
        term = gate * _dot(y_ref[...], wup_ref[n])
        merged = term if merged is None else merged + term
    out = _dot(merged.astype(BF16), wout_ref[...])
    o_ref[...] = x + _rms(out, gpost_ref[...])


def _post(x2, ys, gpre, gpost, wg, bg, wup, wout):
    t, d = x2.shape
    row = lambda i: (i, 0)
    return pl.pallas_call(
        _post_body,
        grid=(t // POST_TILE,),
        in_specs=[pl.BlockSpec((POST_TILE, d), row)]
                 + [pl.BlockSpec((POST_TILE, MIX_W), row) for _ in ys]
                 + [_spec(a) for a in (gpre, gpost, wg, bg, wup, wout)],
        out_specs=pl.BlockSpec((POST_TILE, d), row),
        out_shape=jax.ShapeDtypeStruct((t, d), F32),
        compiler_params=_params("arbitrary"),
        name="mix_post",
    )(x2, *ys, *[_arr(a) for a in (gpre, gpost, wg, bg, wup, wout)])


def _kv_body(m_ref, g_ref, w_ref, k_ref, v_ref):
    d = m_ref.shape[-1]
    m = _rms(m_ref[...], g_ref[...]).astype(BF16)
    k_ref[...] = _dot(m, w_ref[:, 0:d]).astype(BF16)
    v_ref[...] = _dot(m, w_ref[:, d:2 * d]).astype(BF16)


def _kv_proj(mem2, g, w):
    t, d = mem2.shape
    tile = min(ROW_TILE, t)
    row = lambda i: (i, 0)
    return pl.pallas_call(
        _kv_body,
        grid=(t // tile,),
        in_specs=[pl.BlockSpec((tile, d), row), _spec(g), _spec(w)],
        out_specs=[pl.BlockSpec((tile, d), row), pl.BlockSpec((tile, d), row)],
        out_shape=[jax.ShapeDtypeStruct((t, d), BF16), jax.ShapeDtypeStruct((t, d), BF16)],
        compiler_params=_params("arbitrary"),
        name="kv_proj",
    )(mem2, _arr(g), _arr(w))


def _xattn_body(x_ref, k_ref, v_ref, gpre_ref, gpost_ref, wq_ref, wo_ref, o_ref):
    x = x_ref[...]
    d = x.shape[-1]
    dh = d // HEADS
    heads = [slice(hd * dh, (hd + 1) * dh) for hd in range(HEADS)]
    h = _rms(x, gpre_ref[...]).astype(BF16)
    q = _dot(h, wq_ref[...])
    scores = [_dot_nt(q[:, sl].astype(BF16), k_ref[:, sl]) * (dh ** -0.5) for sl in heads]
    probs = []
    for s in scores:
        p = jnp.exp(s - jnp.max(s, axis=-1, keepdims=True))
        probs.append((p / jnp.sum(p, axis=-1, keepdims=True)).astype(BF16))
    outs = [_dot(p, v_ref[:, sl]) for p, sl in zip(probs, heads)]
    o = jnp.concatenate(outs, axis=1).astype(BF16)
    o_ref[...] = x + _rms(_dot(o, wo_ref[...]), gpost_ref[...])


def _xattn(x2, k3, v3, gpre, gpost, wq, wo):
    t, d = x2.shape
    b, m, _ = k3.shape
    assert (t // b) % XATTN_TILE == 0, "a token tile must not straddle two batches' memories"
    per_batch = t // XATTN_TILE // b
    row = lambda i: (i, 0)
    mem = lambda i: (i // per_batch, 0, 0)
    return pl.pallas_call(
        _xattn_body,
        grid=(t // XATTN_TILE,),
        in_specs=[pl.BlockSpec((XATTN_TILE, d), row),
                  pl.BlockSpec((None, m, d), mem), pl.BlockSpec((None, m, d), mem)]
                 + [_spec(a) for a in (gpre, gpost, wq, wo)],
        out_specs=pl.BlockSpec((XATTN_TILE, d), row),
        out_shape=jax.ShapeDtypeStruct((t, d), F32),
        compiler_params=_params("arbitrary"),
        name="xattn",
    )(x2, k3, v3, *[_arr(a) for a in (gpre, gpost, wq, wo)])


def _ffn_body(ff_edges, x_ref, gpre_ref, gpost_ref, wgu_ref, wdn_ref, o_ref):
    x = x_ref[...]
    d_ff = wdn_ref.shape[0]
    h = _rms(x, gpre_ref[...]).astype(BF16)
    acc = None
    for c0, c1 in zip(ff_edges[:-1], ff_edges[1:]):
        gate = _dot(h, wgu_ref[:, c0:c1])
        up = _dot(h, wgu_ref[:, d_ff + c0:d_ff + c1])
        part = _dot((_silu(gate) * up).astype(BF16), wdn_ref[c0:c1, :])
        acc = part if acc is None else acc + part
    o_ref[...] = x + _rms(acc, gpost_ref[...])


def _ffn(x2, gpre, gpost, wgu, wdn):
    t, d = x2.shape
    d_ff = _arr(wdn).shape[-2]
    n_tiles = d_ff // MXU_DIM
    assert n_tiles * MXU_DIM == d_ff
    per_chunk = -(-n_tiles // FFN_CHUNKS)
    ff_edges = tuple(min(i * per_chunk, n_tiles) * MXU_DIM for i in range(FFN_CHUNKS + 1))
    row = lambda i: (i, 0)
    return pl.pallas_call(
        functools.partial(_ffn_body, ff_edges),
        grid=(t // FFN_TILE,),
        in_specs=[pl.BlockSpec((FFN_TILE, d), row)]
                 + [_spec(a) for a in (gpre, gpost, wgu, wdn)],
        out_specs=pl.BlockSpec((FFN_TILE, d), row),
        out_shape=jax.ShapeDtypeStruct((t, d), F32),
        compiler_params=_params("arbitrary"),
        name="ffn",
    )(x2, *[_arr(a) for a in (gpre, gpost, wgu, wdn)])


def _head_norm_gate(o, zg, ng, e_head):
    ms = _dot((o * o).astype(BF16), e_head)
    return o * lax.rsqrt(ms + EPS) * ng * _silu(zg)


def _hgrn2_body(z_ref, lb_ref, ng_ref, le_ref, eh_ref, y_ref, st_ref):
    @pl.when(pl.program_id(1) == 0)
    def _():
        st_ref[...] = jnp.zeros_like(st_ref)

    for bi in range(z_ref.shape[0]):
        _hgrn2_one(z_ref.at[bi], lb_ref, ng_ref, le_ref, eh_ref, y_ref.at[bi], st_ref.at[bi])


def _hgrn2_one(z_ref, lb_ref, ng_ref, le_ref, eh_ref, y_ref, st_ref):
    c = SEQ_BLOCK
    w = MIX_W
    n_chunks = c // HG_CHUNK

    zq = z_ref[:, 0:w]
    zf = z_ref[:, w:2 * w]
    v = z_ref[:, 2 * w:3 * w]
    zg = z_ref[:, 3 * w:4 * w]
    lb = lb_ref[...]
    f = lb + (1.0 - lb) * jax.nn.sigmoid(zf)
    logf = jnp.log(f)
    k = 1.0 - f
    q = _silu(zq)

    le = le_ref[...]
    hi, lo = _split2(logf)
    b = _dot(le, hi) + _dot(le, lo)
    b_last = b.reshape(n_chunks, HG_CHUNK, w)[:, HG_CHUNK - 1:HG_CHUNK, :]
    b_end = jnp.broadcast_to(b_last, (n_chunks, HG_CHUNK, w)).reshape(c, w)
    q_dec = q * jnp.exp(b)
    k_inv = (k * jnp.exp(-b)).astype(BF16)
    k_end = (k * jnp.exp(b_end - b)).astype(BF16)

    lane = lax.broadcasted_iota(jnp.int32, (1, w), 1)
    row = lax.broadcasted_iota(jnp.int32, (c, c), 0)
    col = lax.broadcasted_iota(jnp.int32, (c, c), 1)
    shift = HG_CHUNK.bit_length() - 1
    causal = (row >= col) & ((row >> shift) == (col >> shift))
    hshift = HEAD_W.bit_length() - 1

    o = None
    for hd in range(HEADS):
        mh = (lane >> hshift) == hd
        qh = jnp.where(mh, q_dec, 0.0).astype(BF16)
        a = jnp.where(causal, _dot_nt(qh, k_inv), 0.0).astype(BF16)
        term = _dot(a, jnp.where(mh, v, 0.0).astype(BF16))
        o = term if o is None else o + term

    st = st_ref[...]
    head_bd = (row >> hshift) == (col >> hshift)
    q_bf = q_dec.astype(BF16)
    vt_bf = v.T.astype(BF16)
    kvs = []
    for n in range(n_chunks):
        sl = slice(n * HG_CHUNK, (n + 1) * HG_CHUNK)
        pieces = [k_end[sl]]
        if n > 0:
            pieces.insert(0, jnp.zeros((n * HG_CHUNK, w), BF16))
        if (n + 1) * HG_CHUNK < c:
            pieces.append(jnp.zeros((c - (n + 1) * HG_CHUNK, w), BF16))
        kvs.append(_dot(vt_bf, jnp.concatenate(pieces, axis=0)))
    states = []
    for n in range(n_chunks):
        states.append(st.astype(BF16))
        dec = jnp.exp(b_end[n * HG_CHUNK:n * HG_CHUNK + 1, :])
        st = st * dec + jnp.where(head_bd, kvs[n], 0.0)
    st_ref[...] = st
    parts = []
    for n in range(n_chunks):
        sl = slice(n * HG_CHUNK, (n + 1) * HG_CHUNK)
        parts.append(o[sl] + _dot_nt(q_bf[sl], states[n]))
    o = jnp.concatenate(parts, axis=0)

    y_ref[...] = _head_norm_gate(o, zg, ng_ref[...], eh_ref[...]).astype(BF16)


def _chunk_sum_matrix(c):
    r = jnp.arange(c)
    same = (r[:, None] // HG_CHUNK) == (r[None, :] // HG_CHUNK)
    lower = same & (r[:, None] >= r[None, :])
    return lower.astype(BF16)


def _head_mean_matrix():
    r = jnp.arange(MIX_W)
    same = (r[:, None] // HEAD_W) == (r[None, :] // HEAD_W)
    return (same.astype(F32) / HEAD_W).astype(BF16)


def _hgrn2(z3, lb, ng):
    b, s, _ = z3.shape
    c = SEQ_BLOCK
    le = _chunk_sum_matrix(c)
    eh = _head_mean_matrix()
    nb = HG_BATCH_GROUP
    tok = lambda bi, i: (bi, i, 0)
    return pl.pallas_call(
        _hgrn2_body,
        grid=(b // nb, s // c),
        in_specs=[pl.BlockSpec((nb, c, 4 * MIX_W), tok)]
                 + [_spec(a) for a in (lb, ng, le, eh)],
        out_specs=pl.BlockSpec((nb, c, MIX_W), tok),
        out_shape=jax.ShapeDtypeStruct((b, s, MIX_W), BF16),
        scratch_shapes=[pltpu.VMEM((nb, MIX_W, MIX_W), F32)],
        compiler_params=_params("arbitrary", "arbitrary"),
        name="hgrn2",
    )(z3, *[_arr(a) for a in (lb, ng, le, eh)])


def _ret_body(z_ref, cos_ref, sin_ref, dec_ref, xi_ref, zeta_ref, gend_ref, ng_ref, eh_ref,
              y_ref, st_ref):
    @pl.when(pl.program_id(1) == 0)
    def _():
        st_ref[...] = jnp.zeros_like(st_ref)

    for bi in range(z_ref.shape[0]):
        _ret_one(z_ref.at[bi], cos_ref, sin_ref, dec_ref, xi_ref, zeta_ref, gend_ref, ng_ref, eh_ref,
                 y_ref.at[bi], st_ref.at[bi])


def _ret_one(z_ref, cos_ref, sin_ref, dec_ref, xi_ref, zeta_ref, gend_ref, ng_ref, eh_ref,
             y_ref, st_ref):
    w = MIX_W
    hw = w // 2

    cs = cos_ref[...]
    sn = sin_ref[...]
    q1 = z_ref[:, 0:hw]
    q2 = z_ref[:, hw:w]
    k1 = z_ref[:, w:w + hw]
    k2 = z_ref[:, w + hw:2 * w]
    v = z_ref[:, 2 * w:3 * w]
    zg = z_ref[:, 3 * w:4 * w]
    qr = jnp.concatenate([q1 * cs - q2 * sn, q1 * sn + q2 * cs], axis=1)
    kr = jnp.concatenate([k1 * cs - k2 * sn, k1 * sn + k2 * cs], axis=1) * (HEAD_W ** -0.5)
    kr_bf = kr.astype(BF16)

    lane = lax.broadcasted_iota(jnp.int32, (1, w), 1)
    rshift = (hw // HEADS).bit_length() - 1
    hshift = HEAD_W.bit_length() - 1
    head_rot = (lane & (hw - 1)) >> rshift
    head_nat = lane >> hshift

    o = None
    for hd in range(HEADS):
        qh = jnp.where(head_rot == hd, qr, 0.0).astype(BF16)
        sc = (_dot_nt(qh, kr_bf) * dec_ref[hd]).astype(BF16)
        term = _dot(sc, jnp.where(head_nat == hd, v, 0.0).astype(BF16))
        o = term if o is None else o + term

    st = st_ref[...]
    o = o + _dot_nt((qr * xi_ref[...]).astype(BF16), st.astype(BF16))
    kv = _dot_tn(v.astype(BF16), (kr * zeta_ref[...]).astype(BF16))
    row = lax.broadcasted_iota(jnp.int32, (w, w), 0)
    col = lax.broadcasted_iota(jnp.int32, (w, w), 1)
    head_bd = (row >> hshift) == ((col & (hw - 1)) >> rshift)
    st_ref[...] = st * gend_ref[...] + jnp.where(head_bd, kv, 0.0)

    y_ref[...] = _head_norm_gate(o, zg, ng_ref[...], eh_ref[...]).astype(BF16)


def _rot_lane_head():
    lane = jnp.arange(MIX_W)
    return (lane % (MIX_W // 2)) // (MIX_W // 2 // HEADS)


def _retention(z3, ng):
    b, s, _ = z3.shape
    c = SEQ_BLOCK
    hw = MIX_W // 2
    n_freq = HEAD_W // 2
    pos = jnp.arange(s, dtype=F32)
    inv_freq = ROPE_BASE ** (-jnp.arange(0, HEAD_W, 2, dtype=F32) / HEAD_W)
    ang = pos[:, None] * jnp.tile(inv_freq, hw // n_freq)[None, :]
    cos, sin = jnp.cos(ang), jnp.sin(ang)
    log_gamma = jnp.log1p(-jnp.power(2.0, -5.0 - jnp.arange(HEADS, dtype=F32)))
    idx = jnp.arange(c, dtype=F32)
    rel = idx[:, None] - idx[None, :]
    causal = rel >= 0
    decay = jnp.where(causal, jnp.exp(jnp.where(causal, rel, 0.0)[None] * log_gamma[:, None, None]), 0.0)
    lg_lane = log_gamma[_rot_lane_head()]
    xi = jnp.exp((idx + 1.0)[:, None] * lg_lane[None, :])
    zeta = jnp.exp((c - 1.0 - idx)[:, None] * lg_lane[None, :])
    gend = jnp.exp(c * lg_lane)[None, :]
    eh = _head_mean_matrix()
    nb = RET_BATCH_GROUP
    tok = lambda bi, i: (bi, i, 0)
    return pl.pallas_call(
        _ret_body,
        grid=(b // nb, s // c),
        in_specs=[pl.BlockSpec((nb, c, 4 * MIX_W), tok),
                  pl.BlockSpec((c, hw), lambda bi, i: (i, 0)),
                  pl.BlockSpec((c, hw), lambda bi, i: (i, 0))]
                 + [_spec(a) for a in (decay, xi, zeta, gend, ng, eh)],
        out_specs=pl.BlockSpec((nb, c, MIX_W), tok),
        out_shape=jax.ShapeDtypeStruct((b, s, MIX_W), BF16),
        scratch_shapes=[pltpu.VMEM((nb, MIX_W, MIX_W), F32)],
        compiler_params=_params("arbitrary", "arbitrary"),
        name="retention",
    )(z3, cos, sin, *[_arr(a) for a in (decay, xi, zeta, gend, ng, eh)])


def _to_time_major(src_ref, slab_ref, n_slabs):
    nb, sb, _ = src_ref.shape
    for bi in range(nb):
        for ks in range(n_slabs):
            slab_ref[ks, pl.ds(bi, sb, stride=nb), :] = src_ref[bi, :, ks * LANES:(ks + 1) * LANES]


def _from_time_major(slab_ref, dst_ref, n_slabs):
    nb, sb, _ = dst_ref.shape
    for bi in range(nb):
        cols = [slab_ref[ks, pl.ds(bi, sb, stride=nb), :] for ks in range(n_slabs)]
        dst_ref[bi, :, :] = jnp.concatenate(cols, axis=1).astype(dst_ref.dtype)


def _scan_body(zl_ref, zs_ref, cw_ref, cb_ref, wbd_ref, bias_ref, c_ref,
               lre_ref, lim_ref, bd_ref, cm_ref, d_ref, gw_ref, gb_ref, yl_ref, ys_ref,
               lslab_ref, sslab_ref, a_ref, u_ref, tail_ref, hl_ref, bu_ref, hre_ref, him_ref):
    nb, sb, _ = zl_ref.shape
    n = nb * sb
    half = n // 2
    halves = (slice(0, half), slice(half, n))
    w = MIX_W
    ns = S5_LANES
    halo = (CONV_W - 1) * nb

    @pl.when(pl.program_id(0) == 0)
    def _():
        tail_ref[...] = jnp.zeros_like(tail_ref)
        hl_ref[...] = jnp.zeros_like(hl_ref)
        hre_ref[...] = jnp.zeros_like(hre_ref)
        him_ref[...] = jnp.zeros_like(him_ref)

    _to_time_major(zs_ref, sslab_ref, w // LANES)
    _to_time_major(zl_ref, lslab_ref, 2 * w // LANES)

    su = jnp.concatenate([sslab_ref[0], sslab_ref[1]], axis=1)
    su_bf = su.astype(BF16)
    xg = jnp.concatenate([lslab_ref[0], lslab_ref[1]], axis=1)
    xi = jnp.concatenate([lslab_ref[2], lslab_ref[3]], axis=1)
    xpad = jnp.concatenate([tail_ref[...], xi], axis=0)
    tail_ref[...] = xi[n - halo:n]
    for rows in halves:
        bu_ref[rows, :] = _dot(su_bf[rows], bd_ref[...])
        xc = cb_ref[...]
        for j in range(CONV_W):
            xc = xc + cw_ref[j:j + 1, :] * xpad[rows.start + j * nb:rows.stop + j * nb]
        gates = jax.nn.sigmoid(_dot(xc.astype(BF16), wbd_ref[...]) + bias_ref[...])
        r = gates[:, 0:w]
        ig = gates[:, w:2 * w]
        a = jnp.exp(c_ref[...] * r)
        a_ref[rows, :] = a
        u_ref[rows, :] = jnp.sqrt(1.0 - a * a) * (ig * xc)

    lre = jnp.broadcast_to(lre_ref[...], (nb, ns))
    lim = jnp.broadcast_to(lim_ref[...], (nb, ns))

    def step(t, carry):
        hl, hre, him = carry
        i = pl.multiple_of(t * nb, nb)
        nre = lre * hre - lim * him + bu_ref[pl.ds(i, nb), 0:ns]
        nim = lre * him + lim * hre + bu_ref[pl.ds(i, nb), ns:2 * ns]
        bu_ref[pl.ds(i, nb), 0:ns] = nre
        bu_ref[pl.ds(i, nb), ns:2 * ns] = nim
        hl = a_ref[pl.ds(i, nb), :] * hl + u_ref[pl.ds(i, nb), :]
        u_ref[pl.ds(i, nb), :] = hl
        return hl, nre, nim

    hl, hre, him = lax.fori_loop(0, sb, step, (hl_ref[...], hre_ref[...], him_ref[...]), unroll=2)
    hl_ref[...] = hl
    hre_ref[...] = hre
    him_ref[...] = him

    ys = []
    for rows in halves:
        ys.append(_dot(bu_ref[rows, :].astype(BF16), cm_ref[...]))
        lru_out = u_ref[rows, :] * _gelu_tanh(xg[rows])
        lslab_ref[0, rows, :] = lru_out[:, 0:LANES]
        lslab_ref[1, rows, :] = lru_out[:, LANES:2 * LANES]
    acts = [_gelu_tanh(y + d_ref[...] * su[rows]) for y, rows in zip(ys, halves)]
    glus = [_dot(act.astype(BF16), gw_ref[...]) for act in acts]
    for rows, act, glu in zip(halves, acts, glus):
        out = act * jax.nn.sigmoid(glu + gb_ref[...])
        sslab_ref[0, rows, :] = out[:, 0:LANES]
        sslab_ref[1, rows, :] = out[:, LANES:2 * LANES]
    _from_time_major(lslab_ref, yl_ref, w // LANES)
    _from_time_major(sslab_ref, ys_ref, w // LANES)


def _scan_mixers(zl3, zs3, lru_consts, s5_consts):
    nb, s, _ = zl3.shape
    sb = SCAN_BLOCK
    n = nb * sb
    assert nb == SUBLANES
    blk = lambda i: (0, i, 0)
    consts = tuple(lru_consts) + tuple(s5_consts)
    y_spec = pl.BlockSpec((nb, sb, MIX_W), blk)
    y_shape = jax.ShapeDtypeStruct((nb, s, MIX_W), BF16)
    return pl.pallas_call(
        _scan_body,
        grid=(s // sb,),
        in_specs=[pl.BlockSpec((nb, sb, 2 * MIX_W), blk), pl.BlockSpec((nb, sb, MIX_W), blk)]
                 + [_spec(a) for a in consts],
        out_specs=[y_spec, y_spec],
        out_shape=[y_shape, y_shape],
        scratch_shapes=[pltpu.VMEM((2 * MIX_W // LANES, n, LANES), F32),
                        pltpu.VMEM((MIX_W // LANES, n, LANES), F32),
                        pltpu.VMEM((n, MIX_W), F32),
                        pltpu.VMEM((n, MIX_W), F32),
                        pltpu.VMEM(((CONV_W - 1) * nb, MIX_W), F32),
                        pltpu.VMEM((nb, MIX_W), F32),
                        pltpu.VMEM((n, 2 * S5_LANES), F32),
                        pltpu.VMEM((nb, S5_LANES), F32),
                        pltpu.VMEM((nb, S5_LANES), F32)],
        compiler_params=_params("arbitrary"),
        name="scan_mixers",
    )(zl3, zs3, *[_arr(a) for a in consts])


def _block_diag(blocks):
    g, r, c = blocks.shape[-3:]
    eye = jnp.eye(g, dtype=blocks.dtype)
    out = jnp.einsum('gh,...grc->...grhc', eye, blocks)
    return out.reshape(blocks.shape[:-3] + (g * r, g * c))


def _rotary_column_perm():
    e = jnp.arange(2)[:, None, None]
    h = jnp.arange(HEADS)[None, :, None]
    i = jnp.arange(HEAD_W // 2)[None, None, :]
    return (h * HEAD_W + 2 * i + e).reshape(-1)


def _prep_w_rot(w_in):
    perm = _rotary_column_perm()
    base = 4 * MIX_W
    cols = jnp.concatenate([base + perm, base + MIX_W + perm])
    return w_in[..., cols].astype(BF16)


def _prep_s5(lam_re, lam_im, b_re, b_im, c_re, c_im, log_dt):
    step = jnp.exp(log_dt)[..., None]
    mag = jnp.exp(lam_re * step)
    lb_re = mag * jnp.cos(lam_im * step)
    lb_im = mag * jnp.sin(lam_im * step)
    den = lam_re * lam_re + lam_im * lam_im
    f_re = ((lb_re - 1.0) * lam_re + lb_im * lam_im) / den
    f_im = (lb_im * lam_re - (lb_re - 1.0) * lam_im) / den
    bb_re = f_re[..., None] * b_re - f_im[..., None] * b_im
    bb_im = f_re[..., None] * b_im + f_im[..., None] * b_re
    bd = jnp.concatenate([_block_diag(jnp.swapaxes(bb_re, -1, -2)),
                          _block_diag(jnp.swapaxes(bb_im, -1, -2))], axis=-1)
    cm = jnp.concatenate([_block_diag(jnp.swapaxes(c_re, -1, -2)),
                          -_block_diag(jnp.swapaxes(c_im, -1, -2))], axis=-2)
    n_layers = lam_re.shape[0]
    return (lb_re.reshape(n_layers, 1, -1), lb_im.reshape(n_layers, 1, -1),
            bd.astype(BF16), cm.astype(BF16))


def _rows(v):
    return v.reshape(v.shape[0], 1, -1).astype(F32)


def kernel(x, mem, hg_lower_bounds, norm_mix_pre, norm_mix_post, w_in, w_gate, b_gate, hg_norm, ret_norm, lru_conv_w, lru_conv_b, lru_wa, lru_ba, lru_wx, lru_bx, lru_lambda, s5_lam_re, s5_lam_im, s5_b_re, s5_b_im, s5_c_re, s5_c_im, s5_d, s5_log_dt, s5_glu_w, s5_glu_b, w_up, w_out, norm_xa_pre, norm_xa_post, norm_mem, xa_w_q, xa_w_kv, xa_w_o, norm_ffn_pre, norm_ffn_post, ffn_w_gu, ffn_w_down):
    b, s, d = x.shape
    depth = w_in.shape[0]
    t = b * s

    p = jax.nn.softmax(hg_lower_bounds.astype(F32), axis=0)
    lower_bounds = _rows(jnp.cumsum(p, axis=0) - p[0:1])
    w_in_bf, w_rot = w_in.astype(BF16), _prep_w_rot(w_in)
    w_gate_bf, w_up_bf, w_out_bf = w_gate.astype(BF16), w_up.astype(BF16), w_out.astype(BF16)
    w_q_bf, w_kv_bf, w_o_bf = xa_w_q.astype(BF16), xa_w_kv.astype(BF16), xa_w_o.astype(BF16)
    w_gu_bf, w_down_bf = ffn_w_gu.astype(BF16), ffn_w_down.astype(BF16)
    glu_w_bf = s5_glu_w.astype(BF16)
    lru_wbd = jnp.concatenate([_block_diag(lru_wa), _block_diag(lru_wx)], axis=-1).astype(BF16)
    lru_bias = _rows(jnp.concatenate([lru_ba, lru_bx], axis=-1))
    lru_c = _rows(-LRU_C * jax.nn.softplus(-lru_lambda.astype(F32)))
    lre, lim, bd, cm = _prep_s5(s5_lam_re, s5_lam_im, s5_b_re, s5_b_im, s5_c_re, s5_c_im, s5_log_dt)
    rows = {name: _rows(v) for name, v in dict(
        mix_pre=norm_mix_pre, mix_post=norm_mix_post, b_gate=b_gate, hg_norm=hg_norm,
        ret_norm=ret_norm, conv_b=lru_conv_b, s5_d=s5_d, glu_b=s5_glu_b, xa_pre=norm_xa_pre,
        xa_post=norm_xa_post, mem=norm_mem, ffn_pre=norm_ffn_pre, ffn_post=norm_ffn_post).items()}
    conv_w = lru_conv_w.astype(F32)
    mem2 = mem.reshape(-1, d)

    x2 = x.reshape(t, d)
    for l in range(depth):
        at = lambda stacked: (stacked, l)
        z_hg, z_ret, z_lru, z_s5 = _in_proj(x2, at(rows["mix_pre"]), at(w_in_bf), at(w_rot))
        y_a = _hgrn2(z_hg.reshape(b, s, -1), at(lower_bounds), at(rows["hg_norm"]))
        y_b = _retention(z_ret.reshape(b, s, -1), at(rows["ret_norm"]))
        y_c, y_d = _scan_mixers(
            z_lru.reshape(b, s, -1), z_s5.reshape(b, s, -1),
            (at(conv_w), at(rows["conv_b"]), at(lru_wbd), at(lru_bias), at(lru_c)),
            (at(lre), at(lim), at(bd), at(cm), at(rows["s5_d"]), at(glu_w_bf), at(rows["glu_b"])))
        ys = [y.reshape(t, MIX_W) for y in (y_a, y_b, y_c, y_d)]
        x2 = _post(x2, ys, at(rows["mix_pre"]), at(rows["mix_post"]), at(w_gate_bf),
                   at(rows["b_gate"]), at(w_up_bf), at(w_out_bf))

        k2, v2 = _kv_proj(mem2, at(rows["mem"]), at(w_kv_bf))
        x2 = _xattn(x2, k2.reshape(b, -1, d), v2.reshape(b, -1, d),
                    at(rows["xa_pre"]), at(rows["xa_post"]), at(w_q_bf), at(w_o_bf))
        x2 = _ffn(x2, at(rows["ffn_pre"]), at(rows["ffn_post"]), at(w_gu_bf), at(w_down_bf))
    return x2.reshape(b, s, d)
```

```python
import functools
import math

import jax
import jax.numpy as jnp
from jax import lax
from jax.experimental import pallas as pl
from jax.experimental.pallas import tpu as pltpu

F32 = jnp.float32
BF16 = jnp.bfloat16

EPS = 1e-6
LANES = 128
SUBLANES = 8
MXU_DIM = 256
VMEM_LIMIT = 56 * 1024 * 1024

MIX_W = 256
HEADS = 4
HEAD_W = MIX_W // HEADS
HG_CHUNK = 32
SEQ_BLOCK = 256
HG_BATCH_GROUP = 2
RET_BATCH_GROUP = 4
ROPE_BASE = 10000.0
LRU_C = 8.0
CONV_W = 4
S5_GROUPS = 16
S5_P = 16
S5_N = 64
S5_LANES = S5_GROUPS * S5_N
ROW_TILE = 512
IN_TILE = 1024
POST_TILE = 1024
XATTN_TILE = 1024
FFN_TILE = 1024
FFN_CHUNKS = 3
SCAN_BLOCK = 128


def _dot(a, b):
    return jnp.dot(a, b, preferred_element_type=F32)


def _dot_nt(a, b):
    return lax.dot_general(a, b, (((1,), (1,)), ((), ())), preferred_element_type=F32)


def _dot_tn(a, b):
    return lax.dot_general(a, b, (((0,), (0,)), ((), ())), preferred_element_type=F32)


def _rms(x, g):
    return x * lax.rsqrt(jnp.mean(x * x, axis=-1, keepdims=True) + EPS) * g


def _silu(x):
    return x * jax.nn.sigmoid(x)


def _gelu_tanh(x):
    return 0.5 * x * (1.0 + jnp.tanh(math.sqrt(2.0 / math.pi) * (x + 0.044715 * (x * x * x))))


def _split2(x):
    hi = x.astype(BF16)
    lo = (x - hi.astype(F32)).astype(BF16)
    return hi, lo


def _params(*sem):
    return pltpu.CompilerParams(dimension_semantics=sem, vmem_limit_bytes=VMEM_LIMIT)


def _spec(p):
    if isinstance(p, tuple):
        arr, layer = p
        shape = arr.shape[1:]
        return pl.BlockSpec((None,) + shape, lambda *_: (layer,) + (0,) * len(shape),
                            pipeline_mode=pl.Buffered(1))
    return pl.BlockSpec(p.shape, lambda *_: (0,) * p.ndim, pipeline_mode=pl.Buffered(1))


def _arr(p):
    return p[0] if isinstance(p, tuple) else p


def _in_proj_body(x_ref, g_ref, w_ref, wrot_ref, hg_ref, ret_ref, lru_ref, s5_ref):
    w = MIX_W
    h = _rms(x_ref[...], g_ref[...]).astype(BF16)
    hg_ref[...] = _dot(h, w_ref[:, 0:4 * w]).astype(BF16)
    ret_ref[:, 0:2 * w] = _dot(h, wrot_ref[...]).astype(BF16)
    ret_ref[:, 2 * w:4 * w] = _dot(h, w_ref[:, 6 * w:8 * w]).astype(BF16)
    lru_ref[...] = _dot(h, w_ref[:, 8 * w:10 * w]).astype(BF16)
    s5_ref[...] = _dot(h, w_ref[:, 10 * w:11 * w]).astype(BF16)


def _in_proj(x2, g, w, wrot):
    t, d = x2.shape
    widths = (4 * MIX_W, 4 * MIX_W, 2 * MIX_W, MIX_W)
    return pl.pallas_call(
        _in_proj_body,
        grid=(t // IN_TILE,),
        in_specs=[pl.BlockSpec((IN_TILE, d), lambda i: (i, 0)), _spec(g), _spec(w), _spec(wrot)],
        out_specs=[pl.BlockSpec((IN_TILE, wd), lambda i: (i, 0)) for wd in widths],
        out_shape=[jax.ShapeDtypeStruct((t, wd), BF16) for wd in widths],
        compiler_params=_params("arbitrary"),
        name="in_proj",
    )(x2, _arr(g), _arr(w), _arr(wrot))


def _post_body(x_ref, y0_ref, y1_ref, y2_ref, y3_ref, gpre_ref, gpost_ref, wg_ref, bg_ref,
               wup_ref, wout_ref, o_ref):
    x = x_ref[...]
    d = x.shape[-1]
    h = _rms(x, gpre_ref[...]).astype(BF16)
    merged = None
    for n, y_ref in enumerate((y0_ref, y1_ref, y2_ref, y3_ref)):
        gate = jax.nn.sigmoid(_dot(h, wg_ref[:, n * d:(n + 1) * d]) + bg_ref[:, n * d:(n + 1) * d])
        term = gate * _dot(y_ref[...], wup_ref[n])
        merged = term if merged is None else merged + term
    out = _dot(merged.astype(BF16), wout_ref[...])
    o_ref[...] = x + _rms(out, gpost_ref[...])


def _post(x2, ys, gpre, gpost, wg, bg, wup, wout):
    t, d = x2.shape
    row = lambda i: (i, 0)
    return pl.pallas_call(
        _post_body,
        grid=(t // POST_TILE,),
        in_specs=[pl.BlockSpec((POST_TILE, d), row)]
                 + [pl.BlockSpec((POST_TILE, MIX_W), row) for _ in ys]
                 + [_spec(a) for a in (gpre, gpost, wg, bg, wup, wout)],
        out_specs=pl.BlockSpec((POST_TILE, d), row),
        out_shape=jax.ShapeDtypeStruct((t, d), F32),
        compiler_params=_params("arbitrary"),
        name="mix_post",
    )(x2, *ys, *[_arr(a) for a in (gpre, gpost, wg, bg, wup, wout)])


def _kv_body(m_ref, g_ref, w_ref, k_ref, v_ref):
    d = m_ref.shape[-1]
    m = _rms(m_ref[...], g_ref[...]).astype(BF16)
    k_ref[...] = _dot(m, w_ref[:, 0:d]).astype(BF16)
    v_ref[...] = _dot(m, w_ref[:, d:2 * d]).astype(BF16)


def _kv_proj(mem2, g, w):
    t, d = mem2.shape
    tile = min(ROW_TILE, t)
    row = lambda i: (i, 0)
    return pl.pallas_call(
        _kv_body,
        grid=(t // tile,),
        in_specs=[pl.BlockSpec((tile, d), row), _spec(g), _spec(w)],
        out_specs=[pl.BlockSpec((tile, d), row), pl.BlockSpec((tile, d), row)],
        out_shape=[jax.ShapeDtypeStruct((t, d), BF16), jax.ShapeDtypeStruct((t, d), BF16)],
        compiler_params=_params("arbitrary"),
        name="kv_proj",
    )(mem2, _arr(g), _arr(w))


def _xattn_body(x_ref, k_ref, v_ref, gpre_ref, gpost_ref, wq_ref, wo_ref, o_ref):
    x = x_ref[...]
    d = x.shape[-1]
    dh = d // HEADS
    heads = [slice(hd * dh, (hd + 1) * dh) for hd in range(HEADS)]
    h = _rms(x, gpre_ref[...]).astype(BF16)
    q = _dot(h, wq_ref[...])
    scores = [_dot_nt(q[:, sl].astype(BF16), k_ref[:, sl]) * (dh ** -0.5) for sl in heads]
    probs = []
    for s in scores:
        p = jnp.exp(s - jnp.max(s, axis=-1, keepdims=True))
        probs.append((p / jnp.sum(p, axis=-1, keepdims=True)).astype(BF16))
    outs = [_dot(p, v_ref[:, sl]) for p, sl in zip(probs, heads)]
    o = jnp.concatenate(outs, axis=1).astype(BF16)
    o_ref[...] = x + _rms(_dot(o, wo_ref[...]), gpost_ref[...])


def _xattn(x2, k3, v3, gpre, gpost, wq, wo):
    t, d = x2.shape
    b, m, _ = k3.shape
    assert (t // b) % XATTN_TILE == 0, "a token tile must not straddle two batches' memories"
    per_batch = t // XATTN_TILE // b
    row = lambda i: (i, 0)
    mem = lambda i: (i // per_batch, 0, 0)
    return pl.pallas_call(
        _xattn_body,
        grid=(t // XATTN_TILE,),
        in_specs=[pl.BlockSpec((XATTN_TILE, d), row),
                  pl.BlockSpec((None, m, d), mem), pl.BlockSpec((None, m, d), mem)]
                 + [_spec(a) for a in (gpre, gpost, wq, wo)],
        out_specs=pl.BlockSpec((XATTN_TILE, d), row),
        out_shape=jax.ShapeDtypeStruct((t, d), F32),
        compiler_params=_params("arbitrary"),
        name="xattn",
    )(x2, k3, v3, *[_arr(a) for a in (gpre, gpost, wq, wo)])


def _ffn_body(ff_edges, x_ref, gpre_ref, gpost_ref, wgu_ref, wdn_ref, o_ref):
    x = x_ref[...]
    d_ff = wdn_ref.shape[0]
    h = _rms(x, gpre_ref[...]).astype(BF16)
    acc = None
    for c0, c1 in zip(ff_edges[:-1], ff_edges[1:]):
        gate = _dot(h, wgu_ref[:, c0:c1])
        up = _dot(h, wgu_ref[:, d_ff + c0:d_ff + c1])
        part = _dot((_silu(gate) * up).astype(BF16), wdn_ref[c0:c1, :])
        acc = part if acc is None else acc + part
    o_ref[...] = x + _rms(acc, gpost_ref[...])


def _ffn(x2, gpre, gpost, wgu, wdn):
    t, d = x2.shape
    d_ff = _arr(wdn).shape[-2]
    n_tiles = d_ff // MXU_DIM
    assert n_tiles * MXU_DIM == d_ff
    per_chunk = -(-n_tiles // FFN_CHUNKS)
    ff_edges = tuple(min(i * per_chunk, n_tiles) * MXU_DIM for i in range(FFN_CHUNKS + 1))
    row = lambda i: (i, 0)
    return pl.pallas_call(
        functools.partial(_ffn_body, ff_edges),
        grid=(t // FFN_TILE,),
        in_specs=[pl.BlockSpec((FFN_TILE, d), row)]
                 + [_spec(a) for a in (gpre, gpost, wgu, wdn)],
        out_specs=pl.BlockSpec((FFN_TILE, d), row),
        out_shape=jax.ShapeDtypeStruct((t, d), F32),
        compiler_params=_params("arbitrary"),
        name="ffn",
    )(x2, *[_arr(a) for a in (gpre, gpost, wgu, wdn)])


def _head_norm_gate(o, zg, ng, e_head):
    ms = _dot((o * o).astype(BF16), e_head)
    return o * lax.rsqrt(ms + EPS) * ng * _silu(zg)


def _hgrn2_body(z_ref, lb_ref, ng_ref, le_ref, eh_ref, y_ref, st_ref):
    @pl.when(pl.program_id(1) == 0)
    def _():
        st_ref[...] = jnp.zeros_like(st_ref)

    for bi in range(z_ref.shape[0]):
        _hgrn2_one(z_ref.at[bi], lb_ref, ng_ref, le_ref, eh_ref, y_ref.at[bi], st_ref.at[bi])


def _hgrn2_one(z_ref, lb_ref, ng_ref, le_ref, eh_ref, y_ref, st_ref):
    c = SEQ_BLOCK
    w = MIX_W
    n_chunks = c // HG_CHUNK

    zq = z_ref[:, 0:w].astype(F32)
    zf = z_ref[:, w:2 * w].astype(F32)
    v_bf = z_ref[:, 2 * w:3 * w]
    zg = z_ref[:, 3 * w:4 * w].astype(F32)
    lb = lb_ref[...]
    f = lb + (1.0 - lb) * jax.nn.sigmoid(zf)
    logf = jnp.log(f)
    k = 1.0 - f
    q = _silu(zq)

    le = le_ref[...]
    hi, lo = _split2(logf)
    b = _dot(le, hi) + _dot(le, lo)
    b_last = b.reshape(n_chunks, HG_CHUNK, w)[:, HG_CHUNK - 1:HG_CHUNK, :]
    b_end = jnp.broadcast_to(b_last, (n_chunks, HG_CHUNK, w)).reshape(c, w)
    q_dec = q * jnp.exp(b)
    k_inv = (k * jnp.exp(-b)).astype(BF16)
    k_end = (k * jnp.exp(b_end - b)).astype(BF16)

    lane = lax.broadcasted_iota(jnp.int32, (1, w), 1)
    row = lax.broadcasted_iota(jnp.int32, (c, c), 0)
    col = lax.broadcasted_iota(jnp.int32, (c, c), 1)
    shift = HG_CHUNK.bit_length() - 1
    causal = (row >= col) & ((row >> shift) == (col >> shift))
    hshift = HEAD_W.bit_length() - 1

    q_bf = q_dec.astype(BF16)
    o = None
    for hd in range(HEADS):
        mh = jnp.where((lane >> hshift) == hd, 1.0, 0.0).astype(BF16)
        a = jnp.where(causal, _dot_nt(q_bf * mh, k_inv), 0.0).astype(BF16)
        term = _dot(a, v_bf * mh)
        o = term if o is None else o + term

    st = st_ref[...]
    head_bd = (row >> hshift) == (col >> hshift)
    vt_bf = v_bf.astype(F32).T.astype(BF16)
    kvs = []
    for n in range(n_chunks):
        sl = slice(n * HG_CHUNK, (n + 1) * HG_CHUNK)
        pieces = [k_end[sl]]
        if n > 0:
            pieces.insert(0, jnp.zeros((n * HG_CHUNK, w), BF16))
        if (n + 1) * HG_CHUNK < c:
            pieces.append(jnp.zeros((c - (n + 1) * HG_CHUNK, w), BF16))
        kvs.append(_dot(vt_bf, jnp.concatenate(pieces, axis=0)))
    states = []
    for n in range(n_chunks):
        states.append(st.astype(BF16))
        dec = jnp.exp(b_end[n * HG_CHUNK:n * HG_CHUNK + 1, :])
        st = st * dec + jnp.where(head_bd, kvs[n], 0.0)
    st_ref[...] = st
    parts = []
    for n in range(n_chunks):
        sl = slice(n * HG_CHUNK, (n + 1) * HG_CHUNK)
        parts.append(o[sl] + _dot_nt(q_bf[sl], states[n]))
    o = jnp.concatenate(parts, axis=0)

    y_ref[...] = _head_norm_gate(o, zg, ng_ref[...], eh_ref[...]).astype(BF16)


def _chunk_sum_matrix(c):
    r = jnp.arange(c)
    same = (r[:, None] // HG_CHUNK) == (r[None, :] // HG_CHUNK)
    lower = same & (r[:, None] >= r[None, :])
    return lower.astype(BF16)


def _head_mean_matrix():
    r = jnp.arange(MIX_W)
    same = (r[:, None] // HEAD_W) == (r[None, :] // HEAD_W)
    return (same.astype(F32) / HEAD_W).astype(BF16)


def _hgrn2(z3, lb, ng):
    b, s, _ = z3.shape
    c = SEQ_BLOCK
    le = _chunk_sum_matrix(c)
    eh = _head_mean_matrix()
    nb = HG_BATCH_GROUP
    tok = lambda bi, i: (bi, i, 0)
    return pl.pallas_call(
        _hgrn2_body,
        grid=(b // nb, s // c),
        in_specs=[pl.BlockSpec((nb, c, 4 * MIX_W), tok)]
                 + [_spec(a) for a in (lb, ng, le, eh)],
        out_specs=pl.BlockSpec((nb, c, MIX_W), tok),
        out_shape=jax.ShapeDtypeStruct((b, s, MIX_W), BF16),
        scratch_shapes=[pltpu.VMEM((nb, MIX_W, MIX_W), F32)],
        compiler_params=_params("arbitrary", "arbitrary"),
        name="hgrn2",
    )(z3, *[_arr(a) for a in (lb, ng, le, eh)])


def _ret_body(z_ref, cos_ref, sin_ref, dec_ref, xi_ref, zeta_ref, gend_ref, ng_ref, eh_ref,
              y_ref, st_ref):
    @pl.when(pl.program_id(1) == 0)
    def _():
        st_ref[...] = jnp.zeros_like(st_ref)

    for bi in range(z_ref.shape[0]):
        _ret_one(z_ref.at[bi], cos_ref, sin_ref, dec_ref, xi_ref, zeta_ref, gend_ref, ng_ref, eh_ref,
                 y_ref.at[bi], st_ref.at[bi])


def _ret_one(z_ref, cos_ref, sin_ref, dec_ref, xi_ref, zeta_ref, gend_ref, ng_ref, eh_ref,
             y_ref, st_ref):
    w = MIX_W
    hw = w // 2

    cs = cos_ref[...]
    sn = sin_ref[...]
    q1 = z_ref[:, 0:hw].astype(F32)
    q2 = z_ref[:, hw:w].astype(F32)
    k1 = z_ref[:, w:w + hw].astype(F32)
    k2 = z_ref[:, w + hw:2 * w].astype(F32)
    v_bf = z_ref[:, 2 * w:3 * w]
    zg = z_ref[:, 3 * w:4 * w].astype(F32)
    qr = jnp.concatenate([q1 * cs - q2 * sn, q1 * sn + q2 * cs], axis=1)
    kr = jnp.concatenate([k1 * cs - k2 * sn, k1 * sn + k2 * cs], axis=1) * (HEAD_W ** -0.5)
    kr_bf = kr.astype(BF16)

    lane = lax.broadcasted_iota(jnp.int32, (1, w), 1)
    rshift = (hw // HEADS).bit_length() - 1
    hshift = HEAD_W.bit_length() - 1
    head_rot = (lane & (hw - 1)) >> rshift
    head_nat = lane >> hshift

    qr_bf = qr.astype(BF16)
    o = None
    for hd in range(HEADS):
        m_rot = jnp.where(head_rot == hd, 1.0, 0.0).astype(BF16)
        m_nat = jnp.where(head_nat == hd, 1.0, 0.0).astype(BF16)
        sc = (_dot_nt(qr_bf * m_rot, kr_bf) * dec_ref[hd]).astype(BF16)
        term = _dot(sc, v_bf * m_nat)
        o = term if o is None else o + term

    st = st_ref[...]
    o = o + _dot_nt((qr * xi_ref[...]).astype(BF16), st.astype(BF16))
    kv = _dot_tn(v_bf, (kr * zeta_ref[...]).astype(BF16))
    row = lax.broadcasted_iota(jnp.int32, (w, w), 0)
    col = lax.broadcasted_iota(jnp.int32, (w, w), 1)
    head_bd = (row >> hshift) == ((col & (hw - 1)) >> rshift)
    st_ref[...] = st * gend_ref[...] + jnp.where(head_bd, kv, 0.0)

    y_ref[...] = _head_norm_gate(o, zg, ng_ref[...], eh_ref[...]).astype(BF16)


def _rot_lane_head():
    lane = jnp.arange(MIX_W)
    return (lane % (MIX_W // 2)) // (MIX_W // 2 // HEADS)


def _retention(z3, ng):
    b, s, _ = z3.shape
    c = SEQ_BLOCK
    hw = MIX_W // 2
    n_freq = HEAD_W // 2
    pos = jnp.arange(s, dtype=F32)
    inv_freq = ROPE_BASE ** (-jnp.arange(0, HEAD_W, 2, dtype=F32) / HEAD_W)
    ang = pos[:, None] * jnp.tile(inv_freq, hw // n_freq)[None, :]
    cos, sin = jnp.cos(ang), jnp.sin(ang)
    log_gamma = jnp.log1p(-jnp.power(2.0, -5.0 - jnp.arange(HEADS, dtype=F32)))
    idx = jnp.arange(c, dtype=F32)
    rel = idx[:, None] - idx[None, :]
    causal = rel >= 0
    decay = jnp.where(causal, jnp.exp(jnp.where(causal, rel, 0.0)[None] * log_gamma[:, None, None]), 0.0)
    lg_lane = log_gamma[_rot_lane_head()]
    xi = jnp.exp((idx + 1.0)[:, None] * lg_lane[None, :])
    zeta = jnp.exp((c - 1.0 - idx)[:, None] * lg_lane[None, :])
    gend = jnp.exp(c * lg_lane)[None, :]
    eh = _head_mean_matrix()
    nb = RET_BATCH_GROUP
    tok = lambda bi, i: (bi, i, 0)
    return pl.pallas_call(
        _ret_body,
        grid=(b // nb, s // c),
        in_specs=[pl.BlockSpec((nb, c, 4 * MIX_W), tok),
                  pl.BlockSpec((c, hw), lambda bi, i: (i, 0)),
                  pl.BlockSpec((c, hw), lambda bi, i: (i, 0))]
                 + [_spec(a) for a in (decay, xi, zeta, gend, ng, eh)],
        out_specs=pl.BlockSpec((nb, c, MIX_W), tok),
        out_shape=jax.ShapeDtypeStruct((b, s, MIX_W), BF16),
        scratch_shapes=[pltpu.VMEM((nb, MIX_W, MIX_W), F32)],
        compiler_params=_params("arbitrary", "arbitrary"),
        name="retention",
    )(z3, cos, sin, *[_arr(a) for a in (decay, xi, zeta, gend, ng, eh)])


def _to_time_major(src_ref, slab_ref, n_slabs):
    nb, sb, _ = src_ref.shape
    for bi in range(nb):
        for ks in range(n_slabs):
            slab_ref[ks, pl.ds(bi, sb, stride=nb), :] = (
                src_ref[bi, :, ks * LANES:(ks + 1) * LANES].astype(slab_ref.dtype))


def _from_time_major(slab_ref, dst_ref, n_slabs):
    nb, sb, _ = dst_ref.shape
    for bi in range(nb):
        cols = [slab_ref[ks, pl.ds(bi, sb, stride=nb), :] for ks in range(n_slabs)]
        dst_ref[bi, :, :] = jnp.concatenate(cols, axis=1).astype(dst_ref.dtype)


def _scan_body(zl_ref, zs_ref, cw_ref, cb_ref, wbd_ref, bias_ref, c_ref,
               lre_ref, lim_ref, bd_ref, cm_ref, d_ref, gw_ref, gb_ref, yl_ref, ys_ref,
               lslab_ref, sslab_ref, a_ref, u_ref, tail_ref, hl_ref, bu_ref, hre_ref, him_ref):
    nb, sb, _ = zl_ref.shape
    n = nb * sb
    half = n // 2
    halves = (slice(0, half), slice(half, n))
    w = MIX_W
    ns = S5_LANES
    halo = (CONV_W - 1) * nb

    @pl.when(pl.program_id(0) == 0)
    def _():
        tail_ref[...] = jnp.zeros_like(tail_ref)
        hl_ref[...] = jnp.zeros_like(hl_ref)
        hre_ref[...] = jnp.zeros_like(hre_ref)
        him_ref[...] = jnp.zeros_like(him_ref)

    _to_time_major(zs_ref, sslab_ref, w // LANES)
    _to_time_major(zl_ref, lslab_ref, 2 * w // LANES)

    su = jnp.concatenate([sslab_ref[0], sslab_ref[1]], axis=1)
    su_bf = su.astype(BF16)
    xg = jnp.concatenate([lslab_ref[0], lslab_ref[1]], axis=1)
    xi = jnp.concatenate([lslab_ref[2], lslab_ref[3]], axis=1)
    xpad = jnp.concatenate([tail_ref[...], xi], axis=0)
    tail_ref[...] = xi[n - halo:n]
    for rows in halves:
        bu_ref[rows, :] = _dot(su_bf[rows], bd_ref[...])
        xc = cb_ref[...]
        for j in range(CONV_W):
            xc = xc + cw_ref[j:j + 1, :] * xpad[rows.start + j * nb:rows.stop + j * nb]
        gates = jax.nn.sigmoid(_dot(xc.astype(BF16), wbd_ref[...]) + bias_ref[...])
        r = gates[:, 0:w]
        ig = gates[:, w:2 * w]
        a = jnp.exp(c_ref[...] * r)
        a_ref[rows, :] = a
        u_ref[rows, :] = jnp.sqrt(1.0 - a * a) * (ig * xc)

    lre = jnp.broadcast_to(lre_ref[...], (nb, ns))
    lim = jnp.broadcast_to(lim_ref[...], (nb, ns))

    def step(t, carry):
        hl, hre, him = carry
        i = pl.multiple_of(t * nb, nb)
        nre = lre * hre - lim * him + bu_ref[pl.ds(i, nb), 0:ns]
        nim = lre * him + lim * hre + bu_ref[pl.ds(i, nb), ns:2 * ns]
        bu_ref[pl.ds(i, nb), 0:ns] = nre
        bu_ref[pl.ds(i, nb), ns:2 * ns] = nim
        hl = a_ref[pl.ds(i, nb), :] * hl + u_ref[pl.ds(i, nb), :]
        u_ref[pl.ds(i, nb), :] = hl
        return hl, nre, nim

    hl, hre, him = lax.fori_loop(0, sb, step, (hl_ref[...], hre_ref[...], him_ref[...]), unroll=2)
    hl_ref[...] = hl
    hre_ref[...] = hre
    him_ref[...] = him

    ys = []
    for rows in halves:
        ys.append(_dot(bu_ref[rows, :].astype(BF16), cm_ref[...]))
        lru_out = u_ref[rows, :] * _gelu_tanh(xg[rows])
        lslab_ref[0, rows, :] = lru_out[:, 0:LANES]
        lslab_ref[1, rows, :] = lru_out[:, LANES:2 * LANES]
    acts = [_gelu_tanh(y + d_ref[...] * su[rows]) for y, rows in zip(ys, halves)]
    glus = [_dot(act.astype(BF16), gw_ref[...]) for act in acts]
    for rows, act, glu in zip(halves, acts, glus):
        out = act * jax.nn.sigmoid(glu + gb_ref[...])
        sslab_ref[0, rows, :] = out[:, 0:LANES]
        sslab_ref[1, rows, :] = out[:, LANES:2 * LANES]
    _from_time_major(lslab_ref, yl_ref, w // LANES)
    _from_time_major(sslab_ref, ys_ref, w // LANES)


def _scan_mixers(zl3, zs3, lru_consts, s5_consts):
    nb, s, _ = zl3.shape
    sb = SCAN_BLOCK
    n = nb * sb
    assert nb == SUBLANES
    blk = lambda i: (0, i, 0)
    consts = tuple(lru_consts) + tuple(s5_consts)
    y_spec = pl.BlockSpec((nb, sb, MIX_W), blk)
    y_shape = jax.ShapeDtypeStruct((nb, s, MIX_W), BF16)
    return pl.pallas_call(
        _scan_body,
        grid=(s // sb,),
        in_specs=[pl.BlockSpec((nb, sb, 2 * MIX_W), blk), pl.BlockSpec((nb, sb, MIX_W), blk)]
                 + [_spec(a) for a in consts],
        out_specs=[y_spec, y_spec],
        out_shape=[y_shape, y_shape],
        scratch_shapes=[pltpu.VMEM((2 * MIX_W // LANES, n, LANES), F32),
                        pltpu.VMEM((MIX_W // LANES, n, LANES), F32),
                        pltpu.VMEM((n, MIX_W), F32),
                        pltpu.VMEM((n, MIX_W), F32),
                        pltpu.VMEM(((CONV_W - 1) * nb, MIX_W), F32),
                        pltpu.VMEM((nb, MIX_W), F32),
                        pltpu.VMEM((n, 2 * S5_LANES), F32),
                        pltpu.VMEM((nb, S5_LANES), F32),
                        pltpu.VMEM((nb, S5_LANES), F32)],
        compiler_params=_params("arbitrary"),
        name="scan_mixers",
    )(zl3, zs3, *[_arr(a) for a in consts])


def _block_diag(blocks):
    g, r, c = blocks.shape[-3:]
    eye = jnp.eye(g, dtype=blocks.dtype)
    out = jnp.einsum('gh,...grc->...grhc', eye, blocks)
    return out.reshape(blocks.shape[:-3] + (g * r, g * c))


def _rotary_layout(w):
    lead = w.shape[:-1]
    w = w.reshape(lead + (HEADS, HEAD_W // 2, 2))
    return jnp.moveaxis(w, -1, -3).reshape(lead + (MIX_W,))


def _prep_w_rot(w_in_bf):
    base = 4 * MIX_W
    return jnp.concatenate([_rotary_layout(w_in_bf[..., base:base + MIX_W]),
                            _rotary_layout(w_in_bf[..., base + MIX_W:base + 2 * MIX_W])], axis=-1)


def _prep_s5(lam_re, lam_im, b_re, b_im, c_re, c_im, log_dt):
    step = jnp.exp(log_dt)[..., None]
    mag = jnp.exp(lam_re * step)
    lb_re = mag * jnp.cos(lam_im * step)
    lb_im = mag * jnp.sin(lam_im * step)
    den = lam_re * lam_re + lam_im * lam_im
    f_re = ((lb_re - 1.0) * lam_re + lb_im * lam_im) / den
    f_im = (lb_im * lam_re - (lb_re - 1.0) * lam_im) / den
    bb_re = f_re[..., None] * b_re - f_im[..., None] * b_im
    bb_im = f_re[..., None] * b_im + f_im[..., None] * b_re
    bd = jnp.concatenate([_block_diag(jnp.swapaxes(bb_re, -1, -2)),
                          _block_diag(jnp.swapaxes(bb_im, -1, -2))], axis=-1)
    cm = jnp.concatenate([_block_diag(jnp.swapaxes(c_re, -1, -2)),
                          -_block_diag(jnp.swapaxes(c_im, -1, -2))], axis=-2)
    n_layers = lam_re.shape[0]
    return (lb_re.reshape(n_layers, 1, -1), lb_im.reshape(n_layers, 1, -1),
            bd.astype(BF16), cm.astype(BF16))


def _rows(v):
    return v.reshape(v.shape[0], 1, -1).astype(F32)


def kernel(x, mem, hg_lower_bounds, norm_mix_pre, norm_mix_post, w_in, w_gate, b_gate, hg_norm, ret_norm, lru_conv_w, lru_conv_b, lru_wa, lru_ba, lru_wx, lru_bx, lru_lambda, s5_lam_re, s5_lam_im, s5_b_re, s5_b_im, s5_c_re, s5_c_im, s5_d, s5_log_dt, s5_glu_w, s5_glu_b, w_up, w_out, norm_xa_pre, norm_xa_post, norm_mem, xa_w_q, xa_w_kv, xa_w_o, norm_ffn_pre, norm_ffn_post, ffn_w_gu, ffn_w_down):
    b, s, d = x.shape
    depth = w_in.shape[0]
    t = b * s

    p = jax.nn.softmax(hg_lower_bounds.astype(F32), axis=0)
    lower_bounds = _rows(jnp.cumsum(p, axis=0) - p[0:1])
    w_in_bf = w_in.astype(BF16)
    w_rot = _prep_w_rot(w_in_bf)
    w_gate_bf, w_up_bf, w_out_bf = w_gate.astype(BF16), w_up.astype(BF16), w_out.astype(BF16)
    w_q_bf, w_kv_bf, w_o_bf = xa_w_q.astype(BF16), xa_w_kv.astype(BF16), xa_w_o.astype(BF16)
    w_gu_bf, w_down_bf = ffn_w_gu.astype(BF16), ffn_w_down.astype(BF16)
    glu_w_bf = s5_glu_w.astype(BF16)
    lru_wbd = jnp.concatenate([_block_diag(lru_wa), _block_diag(lru_wx)], axis=-1).astype(BF16)
    lru_bias = _rows(jnp.concatenate([lru_ba, lru_bx], axis=-1))
    lru_c = _rows(-LRU_C * jax.nn.softplus(-lru_lambda.astype(F32)))
    lre, lim, bd, cm = _prep_s5(s5_lam_re, s5_lam_im, s5_b_re, s5_b_im, s5_c_re, s5_c_im, s5_log_dt)
    rows = {name: _rows(v) for name, v in dict(
        mix_pre=norm_mix_pre, mix_post=norm_mix_post, b_gate=b_gate, hg_norm=hg_norm,
        ret_norm=ret_norm, conv_b=lru_conv_b, s5_d=s5_d, glu_b=s5_glu_b, xa_pre=norm_xa_pre,
        xa_post=norm_xa_post, mem=norm_mem, ffn_pre=norm_ffn_pre, ffn_post=norm_ffn_post).items()}
    conv_w = lru_conv_w.astype(F32)
    mem2 = mem.reshape(-1, d)

    x2 = x.reshape(t, d)
    for l in range(depth):
        at = lambda stacked: (stacked, l)
        z_hg, z_ret, z_lru, z_s5 = _in_proj(x2, at(rows["mix_pre"]), at(w_in_bf), at(w_rot))
        y_a = _hgrn2(z_hg.reshape(b, s, -1), at(lower_bounds), at(rows["hg_norm"]))
        y_b = _retention(z_ret.reshape(b, s, -1), at(rows["ret_norm"]))
        y_c, y_d = _scan_mixers(
            z_lru.reshape(b, s, -1), z_s5.reshape(b, s, -1),
            (at(conv_w), at(rows["conv_b"]), at(lru_wbd), at(lru_bias), at(lru_c)),
            (at(lre), at(lim), at(bd), at(cm), at(rows["s5_d"]), at(glu_w_bf), at(rows["glu_b"])))
        ys = [y.reshape(t, MIX_W) for y in (y_a, y_b, y_c, y_d)]
        x2 = _post(x2, ys, at(rows["mix_pre"]), at(rows["mix_post"]), at(w_gate_bf),
                   at(rows["b_gate"]), at(w_up_bf), at(w_out_bf))

        k2, v2 = _kv_proj(mem2, at(rows["mem"]), at(w_kv_bf))
        x2 = _xattn(x2, k2.reshape(b, -1, d), v2.reshape(b, -1, d),
                    at(rows["xa_pre"]), at(rows["xa_post"]), at(w_q_bf), at(w_o_bf))
        x2 = _ffn(x2, at(rows["ffn_pre"]), at(rows["ffn_post"]), at(w_gu_bf), at(w_down_bf))
    return x2.reshape(b, s, d)
```

```python
import functools
import math

import jax
import jax.numpy as jnp
from jax import lax
from jax.experimental import pallas as pl
from jax.experimental.pallas import tpu as pltpu

F32 = jnp.float32
BF16 = jnp.bfloat16

EPS = 1e-6
LANES = 128
SUBLANES = 8
MXU_DIM = 256
VMEM_LIMIT = 56 * 1024 * 1024

MIX_W = 256
HEADS = 4
HEAD_W = MIX_W // HEADS
HG_CHUNK = 32
SEQ_BLOCK = 256
HG_BATCH_GROUP = 2
RET_BATCH_GROUP = 4
ROPE_BASE = 10000.0
LRU_C = 8.0
CONV_W = 4
S5_GROUPS = 16
S5_P = 16
S5_N = 64
S5_LANES = S5_GROUPS * S5_N
ROW_TILE = 512
IN_TILE = 1024
POST_TILE = 1024
XATTN_TILE = 1024
FFN_TILE = 1024
FFN_CHUNKS = 3
SCAN_BLOCK = 128
ROW_CHUNK = 128


def _dot(a, b):
    return jnp.dot(a, b, preferred_element_type=F32)


def _dot_nt(a, b):
    return lax.dot_general(a, b, (((1,), (1,)), ((), ())), preferred_element_type=F32)


def _dot_tn(a, b):
    return lax.dot_general(a, b, (((0,), (0,)), ((), ())), preferred_element_type=F32)


def _rms(x, g):
    return x * lax.rsqrt(jnp.mean(x * x, axis=-1, keepdims=True) + EPS) * g


def _silu(x):
    return x * jax.nn.sigmoid(x)


def _gelu_tanh(x):
    return 0.5 * x * (1.0 + jnp.tanh(math.sqrt(2.0 / math.pi) * (x + 0.044715 * (x * x * x))))


def _split2(x):
    hi = x.astype(BF16)
    lo = (x - hi.astype(F32)).astype(BF16)
    return hi, lo


def _params(*sem):
    return pltpu.CompilerParams(dimension_semantics=sem, vmem_limit_bytes=VMEM_LIMIT)


def _spec(p):
    if isinstance(p, tuple):
        arr, layer = p
        shape = arr.shape[1:]
        return pl.BlockSpec((None,) + shape, lambda *_: (layer,) + (0,) * len(shape),
                            pipeline_mode=pl.Buffered(1))
    return pl.BlockSpec(p.shape, lambda *_: (0,) * p.ndim, pipeline_mode=pl.Buffered(1))


def _arr(p):
    return p[0] if isinstance(p, tuple) else p


def _in_proj_body(x_ref, g_ref, w_ref, wrot_ref, hg_ref, ret_ref, lru_ref, s5_ref):
    w = MIX_W
    h = _rms(x_ref[...], g_ref[...]).astype(BF16)
    hg_ref[...] = _dot(h, w_ref[:, 0:4 * w]).astype(BF16)
    ret_ref[:, 0:2 * w] = _dot(h, wrot_ref[...]).astype(BF16)
    ret_ref[:, 2 * w:4 * w] = _dot(h, w_ref[:, 6 * w:8 * w]).astype(BF16)
    lru_ref[...] = _dot(h, w_ref[:, 8 * w:10 * w]).astype(BF16)
    s5_ref[...] = _dot(h, w_ref[:, 10 * w:11 * w]).astype(BF16)


def _in_proj(x2, g, w, wrot):
    t, d = x2.shape
    widths = (4 * MIX_W, 4 * MIX_W, 2 * MIX_W, MIX_W)
    return pl.pallas_call(
        _in_proj_body,
        grid=(t // IN_TILE,),
        in_specs=[pl.BlockSpec((IN_TILE, d), lambda i: (i, 0)), _spec(g), _spec(w), _spec(wrot)],
        out_specs=[pl.BlockSpec((IN_TILE, wd), lambda i: (i, 0)) for wd in widths],
        out_shape=[jax.ShapeDtypeStruct((t, wd), BF16) for wd in widths],
        compiler_params=_params("arbitrary"),
        name="in_proj",
    )(x2, _arr(g), _arr(w), _arr(wrot))


def _post_body(x_ref, y0_ref, y1_ref, y2_ref, y3_ref, gpre_ref, gpost_ref, wg_ref, bg_ref,
               wup_ref, wout_ref, o_ref):
    x = x_ref[...]
    d = x.shape[-1]
    h = _rms(x, gpre_ref[...]).astype(BF16)
    merged = None
    for n, y_ref in enumerate((y0_ref, y1_ref, y2_ref, y3_ref)):
        gate = jax.nn.sigmoid(_dot(h, wg_ref[:, n * d:(n + 1) * d]) + bg_ref[:, n * d:(n + 1) * d])
        term = gate * _dot(y_ref[...], wup_ref[n])
        merged = term if merged is None else merged + term
    out = _dot(merged.astype(BF16), wout_ref[...])
    o_ref[...] = x + _rms(out, gpost_ref[...])


def _post(x2, ys, gpre, gpost, wg, bg, wup, wout):
    t, d = x2.shape
    row = lambda i: (i, 0)
    return pl.pallas_call(
        _post_body,
        grid=(t // POST_TILE,),
        in_specs=[pl.BlockSpec((POST_TILE, d), row)]
                 + [pl.BlockSpec((POST_TILE, MIX_W), row) for _ in ys]
                 + [_spec(a) for a in (gpre, gpost, wg, bg, wup, wout)],
        out_specs=pl.BlockSpec((POST_TILE, d), row),
        out_shape=jax.ShapeDtypeStruct((t, d), F32),
        compiler_params=_params("arbitrary"),
        name="mix_post",
    )(x2, *ys, *[_arr(a) for a in (gpre, gpost, wg, bg, wup, wout)])


def _kv_body(m_ref, g_ref, w_ref, k_ref, v_ref):
    d = m_ref.shape[-1]
    m = _rms(m_ref[...], g_ref[...]).astype(BF16)
    k_ref[...] = _dot(m, w_ref[:, 0:d]).astype(BF16)
    v_ref[...] = _dot(m, w_ref[:, d:2 * d]).astype(BF16)


def _kv_proj(mem2, g, w):
    t, d = mem2.shape
    tile = min(ROW_TILE, t)
    row = lambda i: (i, 0)
    return pl.pallas_call(
        _kv_body,
        grid=(t // tile,),
        in_specs=[pl.BlockSpec((tile, d), row), _spec(g), _spec(w)],
        out_specs=[pl.BlockSpec((tile, d), row), pl.BlockSpec((tile, d), row)],
        out_shape=[jax.ShapeDtypeStruct((t, d), BF16), jax.ShapeDtypeStruct((t, d), BF16)],
        compiler_params=_params("arbitrary"),
        name="kv_proj",
    )(mem2, _arr(g), _arr(w))


def _xattn_body(x_ref, k_ref, v_ref, gpre_ref, gpost_ref, wq_ref, wo_ref, o_ref):
    x = x_ref[...]
    d = x.shape[-1]
    dh = d // HEADS
    heads = [slice(hd * dh, (hd + 1) * dh) for hd in range(HEADS)]
    h = _rms(x, gpre_ref[...]).astype(BF16)
    q = _dot(h, wq_ref[...])
    scores = [_dot_nt(q[:, sl].astype(BF16), k_ref[:, sl]) * (dh ** -0.5) for sl in heads]
    probs = []
    for s in scores:
        p = jnp.exp(s - jnp.max(s, axis=-1, keepdims=True))
        probs.append((p / jnp.sum(p, axis=-1, keepdims=True)).astype(BF16))
    outs = [_dot(p, v_ref[:, sl]) for p, sl in zip(probs, heads)]
    o = jnp.concatenate(outs, axis=1).astype(BF16)
    o_ref[...] = x + _rms(_dot(o, wo_ref[...]), gpost_ref[...])


def _xattn(x2, k3, v3, gpre, gpost, wq, wo):
    t, d = x2.shape
    b, m, _ = k3.shape
    assert (t // b) % XATTN_TILE == 0, "a token tile must not straddle two batches' memories"
    per_batch = t // XATTN_TILE // b
    row = lambda i: (i, 0)
    mem = lambda i: (i // per_batch, 0, 0)
    return pl.pallas_call(
        _xattn_body,
        grid=(t // XATTN_TILE,),
        in_specs=[pl.BlockSpec((XATTN_TILE, d), row),
                  pl.BlockSpec((None, m, d), mem), pl.BlockSpec((None, m, d), mem)]
                 + [_spec(a) for a in (gpre, gpost, wq, wo)],
        out_specs=pl.BlockSpec((XATTN_TILE, d), row),
        out_shape=jax.ShapeDtypeStruct((t, d), F32),
        compiler_params=_params("arbitrary"),
        name="xattn",
    )(x2, k3, v3, *[_arr(a) for a in (gpre, gpost, wq, wo)])


def _ffn_body(ff_edges, x_ref, gpre_ref, gpost_ref, wgu_ref, wdn_ref, o_ref):
    x = x_ref[...]
    d_ff = wdn_ref.shape[0]
    h = _rms(x, gpre_ref[...]).astype(BF16)
    acc = None
    for c0, c1 in zip(ff_edges[:-1], ff_edges[1:]):
        gate = _dot(h, wgu_ref[:, c0:c1])
        up = _dot(h, wgu_ref[:, d_ff + c0:d_ff + c1])
        part = _dot((_silu(gate) * up).astype(BF16), wdn_ref[c0:c1, :])
        acc = part if acc is None else acc + part
    o_ref[...] = x + _rms(acc, gpost_ref[...])


def _ffn(x2, gpre, gpost, wgu, wdn):
    t, d = x2.shape
    d_ff = _arr(wdn).shape[-2]
    n_tiles = d_ff // MXU_DIM
    assert n_tiles * MXU_DIM == d_ff
    per_chunk = -(-n_tiles // FFN_CHUNKS)
    ff_edges = tuple(min(i * per_chunk, n_tiles) * MXU_DIM for i in range(FFN_CHUNKS + 1))
    row = lambda i: (i, 0)
    return pl.pallas_call(
        functools.partial(_ffn_body, ff_edges),
        grid=(t // FFN_TILE,),
        in_specs=[pl.BlockSpec((FFN_TILE, d), row)]
                 + [_spec(a) for a in (gpre, gpost, wgu, wdn)],
        out_specs=pl.BlockSpec((FFN_TILE, d), row),
        out_shape=jax.ShapeDtypeStruct((t, d), F32),
        compiler_params=_params("arbitrary"),
        name="ffn",
    )(x2, *[_arr(a) for a in (gpre, gpost, wgu, wdn)])


def _head_norm_gate(o, zg, ng, e_head):
    ms = _dot((o * o).astype(BF16), e_head)
    return o * lax.rsqrt(ms + EPS) * ng * _silu(zg)


def _hgrn2_body(z_ref, lb_ref, ng_ref, le_ref, eh_ref, y_ref, st_ref):
    @pl.when(pl.program_id(1) == 0)
    def _():
        st_ref[...] = jnp.zeros_like(st_ref)

    for bi in range(z_ref.shape[0]):
        _hgrn2_one(z_ref.at[bi], lb_ref, ng_ref, le_ref, eh_ref, y_ref.at[bi], st_ref.at[bi])


def _hgrn2_one(z_ref, lb_ref, ng_ref, le_ref, eh_ref, y_ref, st_ref):
    c = SEQ_BLOCK
    w = MIX_W
    n_chunks = c // HG_CHUNK

    zq = z_ref[:, 0:w].astype(F32)
    zf = z_ref[:, w:2 * w].astype(F32)
    v_bf = z_ref[:, 2 * w:3 * w]
    zg = z_ref[:, 3 * w:4 * w].astype(F32)
    lb = lb_ref[...]
    f = lb + (1.0 - lb) * jax.nn.sigmoid(zf)
    logf = jnp.log(f)
    k = 1.0 - f
    q = _silu(zq)

    le = le_ref[...]
    hi, lo = _split2(logf)
    b = _dot(le, hi) + _dot(le, lo)
    b_last = b.reshape(n_chunks, HG_CHUNK, w)[:, HG_CHUNK - 1:HG_CHUNK, :]
    b_end = jnp.broadcast_to(b_last, (n_chunks, HG_CHUNK, w)).reshape(c, w)
    q_dec = q * jnp.exp(b)
    k_inv = (k * jnp.exp(-b)).astype(BF16)
    k_end = (k * jnp.exp(b_end - b)).astype(BF16)

    lane = lax.broadcasted_iota(jnp.int32, (1, w), 1)
    row = lax.broadcasted_iota(jnp.int32, (c, c), 0)
    col = lax.broadcasted_iota(jnp.int32, (c, c), 1)
    shift = HG_CHUNK.bit_length() - 1
    causal = (row >= col) & ((row >> shift) == (col >> shift))
    hshift = HEAD_W.bit_length() - 1

    q_bf = q_dec.astype(BF16)
    o = None
    for hd in range(HEADS):
        mh = jnp.where((lane >> hshift) == hd, 1.0, 0.0).astype(BF16)
        a = jnp.where(causal, _dot_nt(q_bf * mh, k_inv), 0.0).astype(BF16)
        term = _dot(a, v_bf * mh)
        o = term if o is None else o + term

    st = st_ref[...]
    head_bd = (row >> hshift) == (col >> hshift)
    vt_bf = v_bf.astype(F32).T.astype(BF16)
    kvs = []
    for n in range(n_chunks):
        sl = slice(n * HG_CHUNK, (n + 1) * HG_CHUNK)
        pieces = [k_end[sl]]
        if n > 0:
            pieces.insert(0, jnp.zeros((n * HG_CHUNK, w), BF16))
        if (n + 1) * HG_CHUNK < c:
            pieces.append(jnp.zeros((c - (n + 1) * HG_CHUNK, w), BF16))
        kvs.append(_dot(vt_bf, jnp.concatenate(pieces, axis=0)))
    states = []
    for n in range(n_chunks):
        states.append(st.astype(BF16))
        dec = jnp.exp(b_end[n * HG_CHUNK:n * HG_CHUNK + 1, :])
        st = st * dec + jnp.where(head_bd, kvs[n], 0.0)
    st_ref[...] = st
    parts = []
    for n in range(n_chunks):
        sl = slice(n * HG_CHUNK, (n + 1) * HG_CHUNK)
        parts.append(o[sl] + _dot_nt(q_bf[sl], states[n]))
    o = jnp.concatenate(parts, axis=0)

    y_ref[...] = _head_norm_gate(o, zg, ng_ref[...], eh_ref[...]).astype(BF16)


def _chunk_sum_matrix(c):
    r = jnp.arange(c)
    same = (r[:, None] // HG_CHUNK) == (r[None, :] // HG_CHUNK)
    lower = same & (r[:, None] >= r[None, :])
    return lower.astype(BF16)


def _head_mean_matrix():
    r = jnp.arange(MIX_W)
    same = (r[:, None] // HEAD_W) == (r[None, :] // HEAD_W)
    return (same.astype(F32) / HEAD_W).astype(BF16)


def _hgrn2(z3, lb, ng):
    b, s, _ = z3.shape
    c = SEQ_BLOCK
    le = _chunk_sum_matrix(c)
    eh = _head_mean_matrix()
    nb = HG_BATCH_GROUP
    tok = lambda bi, i: (bi, i, 0)
    return pl.pallas_call(
        _hgrn2_body,
        grid=(b // nb, s // c),
        in_specs=[pl.BlockSpec((nb, c, 4 * MIX_W), tok)]
                 + [_spec(a) for a in (lb, ng, le, eh)],
        out_specs=pl.BlockSpec((nb, c, MIX_W), tok),
        out_shape=jax.ShapeDtypeStruct((b, s, MIX_W), BF16),
        scratch_shapes=[pltpu.VMEM((nb, MIX_W, MIX_W), F32)],
        compiler_params=_params("arbitrary", "arbitrary"),
        name="hgrn2",
    )(z3, *[_arr(a) for a in (lb, ng, le, eh)])


def _ret_body(z_ref, cos_ref, sin_ref, dec_ref, xi_ref, zeta_ref, gend_ref, ng_ref, eh_ref,
              y_ref, st_ref):
    @pl.when(pl.program_id(1) == 0)
    def _():
        st_ref[...] = jnp.zeros_like(st_ref)

    for bi in range(z_ref.shape[0]):
        _ret_one(z_ref.at[bi], cos_ref, sin_ref, dec_ref, xi_ref, zeta_ref, gend_ref, ng_ref, eh_ref,
                 y_ref.at[bi], st_ref.at[bi])


def _ret_one(z_ref, cos_ref, sin_ref, dec_ref, xi_ref, zeta_ref, gend_ref, ng_ref, eh_ref,
             y_ref, st_ref):
    w = MIX_W
    hw = w // 2

    cs = cos_ref[...]
    sn = sin_ref[...]
    q1 = z_ref[:, 0:hw].astype(F32)
    q2 = z_ref[:, hw:w].astype(F32)
    k1 = z_ref[:, w:w + hw].astype(F32)
    k2 = z_ref[:, w + hw:2 * w].astype(F32)
    v_bf = z_ref[:, 2 * w:3 * w]
    zg = z_ref[:, 3 * w:4 * w].astype(F32)
    qr = jnp.concatenate([q1 * cs - q2 * sn, q1 * sn + q2 * cs], axis=1)
    kr = jnp.concatenate([k1 * cs - k2 * sn, k1 * sn + k2 * cs], axis=1) * (HEAD_W ** -0.5)
    kr_bf = kr.astype(BF16)

    lane = lax.broadcasted_iota(jnp.int32, (1, w), 1)
    rshift = (hw // HEADS).bit_length() - 1
    hshift = HEAD_W.bit_length() - 1
    head_rot = (lane & (hw - 1)) >> rshift
    head_nat = lane >> hshift

    qr_bf = qr.astype(BF16)
    o = None
    for hd in range(HEADS):
        m_rot = jnp.where(head_rot == hd, 1.0, 0.0).astype(BF16)
        m_nat = jnp.where(head_nat == hd, 1.0, 0.0).astype(BF16)
        sc = (_dot_nt(qr_bf * m_rot, kr_bf) * dec_ref[hd]).astype(BF16)
        term = _dot(sc, v_bf * m_nat)
        o = term if o is None else o + term

    st = st_ref[...]
    o = o + _dot_nt((qr * xi_ref[...]).astype(BF16), st.astype(BF16))
    kv = _dot_tn(v_bf, (kr * zeta_ref[...]).astype(BF16))
    row = lax.broadcasted_iota(jnp.int32, (w, w), 0)
    col = lax.broadcasted_iota(jnp.int32, (w, w), 1)
    head_bd = (row >> hshift) == ((col & (hw - 1)) >> rshift)
    st_ref[...] = st * gend_ref[...] + jnp.where(head_bd, kv, 0.0)

    y_ref[...] = _head_norm_gate(o, zg, ng_ref[...], eh_ref[...]).astype(BF16)


def _rot_lane_head():
    lane = jnp.arange(MIX_W)
    return (lane % (MIX_W // 2)) // (MIX_W // 2 // HEADS)


def _retention(z3, ng):
    b, s, _ = z3.shape
    c = SEQ_BLOCK
    hw = MIX_W // 2
    n_freq = HEAD_W // 2
    pos = jnp.arange(s, dtype=F32)
    inv_freq = ROPE_BASE ** (-jnp.arange(0, HEAD_W, 2, dtype=F32) / HEAD_W)
    ang = pos[:, None] * jnp.tile(inv_freq, hw // n_freq)[None, :]
    cos, sin = jnp.cos(ang), jnp.sin(ang)
    log_gamma = jnp.log1p(-jnp.power(2.0, -5.0 - jnp.arange(HEADS, dtype=F32)))
    idx = jnp.arange(c, dtype=F32)
    rel = idx[:, None] - idx[None, :]
    causal = rel >= 0
    decay = jnp.where(causal, jnp.exp(jnp.where(causal, rel, 0.0)[None] * log_gamma[:, None, None]), 0.0)
    lg_lane = log_gamma[_rot_lane_head()]
    xi = jnp.exp((idx + 1.0)[:, None] * lg_lane[None, :])
    zeta = jnp.exp((c - 1.0 - idx)[:, None] * lg_lane[None, :])
    gend = jnp.exp(c * lg_lane)[None, :]
    eh = _head_mean_matrix()
    nb = RET_BATCH_GROUP
    tok = lambda bi, i: (bi, i, 0)
    return pl.pallas_call(
        _ret_body,
        grid=(b // nb, s // c),
        in_specs=[pl.BlockSpec((nb, c, 4 * MIX_W), tok),
                  pl.BlockSpec((c, hw), lambda bi, i: (i, 0)),
                  pl.BlockSpec((c, hw), lambda bi, i: (i, 0))]
                 + [_spec(a) for a in (decay, xi, zeta, gend, ng, eh)],
        out_specs=pl.BlockSpec((nb, c, MIX_W), tok),
        out_shape=jax.ShapeDtypeStruct((b, s, MIX_W), BF16),
        scratch_shapes=[pltpu.VMEM((nb, MIX_W, MIX_W), F32)],
        compiler_params=_params("arbitrary", "arbitrary"),
        name="retention",
    )(z3, cos, sin, *[_arr(a) for a in (decay, xi, zeta, gend, ng, eh)])


def _to_time_major(src_ref, slab_ref, n_slabs, row0=0):
    nb, sb, _ = src_ref.shape
    for bi in range(nb):
        for ks in range(n_slabs):
            slab_ref[ks, pl.ds(row0 + bi, sb, stride=nb), :] = (
                src_ref[bi, :, ks * LANES:(ks + 1) * LANES].astype(slab_ref.dtype))


def _from_time_major(slab_ref, dst_ref, n_slabs, row0=0):
    nb, sb, _ = dst_ref.shape
    for bi in range(nb):
        cols = [slab_ref[ks, pl.ds(row0 + bi, sb, stride=nb), :] for ks in range(n_slabs)]
        dst_ref[bi, :, :] = jnp.concatenate(cols, axis=1).astype(dst_ref.dtype)


def _row_chunks(n_rows, body):
    for r0 in range(0, n_rows, ROW_CHUNK):
        body(r0)


def _pair(slab_ref, k0, rows):
    return jnp.concatenate([slab_ref[k0, rows, :], slab_ref[k0 + 1, rows, :]], axis=1)


def _scan_body(zl_ref, zs_ref, cw_ref, cb_ref, wbd_ref, bias_ref, c_ref,
               lre_ref, lim_ref, bd_ref, cm_ref, d_ref, gw_ref, gb_ref, yl_ref, ys_ref,
               lslab_ref, sslab_ref, xc_ref, g_ref, a_ref, u_ref, hl_ref,
               bu_ref, y_ref, act_ref, hre_ref, him_ref):
    nb, sb, _ = zl_ref.shape
    n = nb * sb
    rc = ROW_CHUNK
    halves = (slice(0, n // 2), slice(n // 2, n))
    w = MIX_W
    ns = S5_LANES
    halo = (CONV_W - 1) * nb

    @pl.when(pl.program_id(0) == 0)
    def _():
        lslab_ref[:, 0:halo, :] = jnp.zeros((lslab_ref.shape[0], halo, LANES), F32)
        hl_ref[...] = jnp.zeros_like(hl_ref)
        hre_ref[...] = jnp.zeros_like(hre_ref)
        him_ref[...] = jnp.zeros_like(him_ref)

    _to_time_major(zs_ref, sslab_ref, w // LANES)
    _to_time_major(zl_ref, lslab_ref, 2 * w // LANES, row0=halo)

    for rows in halves:
        bu_ref[rows, :] = _dot(_pair(sslab_ref, 0, rows).astype(BF16), bd_ref[...])

    def lru_conv(r0):
        xc = cb_ref[...]
        for j in range(CONV_W):
            xc = xc + cw_ref[j:j + 1, :] * _pair(lslab_ref, 2, pl.ds(r0 + j * nb, rc))
        xc_ref[pl.ds(r0, rc), :] = xc

    _row_chunks(n, lru_conv)
    for rows in halves:
        g_ref[rows, :] = _dot(xc_ref[rows, :].astype(BF16), wbd_ref[...])

    def lru_gates(r0):
        rows = pl.ds(r0, rc)
        g = jax.nn.sigmoid(g_ref[rows, :] + bias_ref[...])
        a = jnp.exp(c_ref[...] * g[:, 0:w])
        a_ref[rows, :] = a
        u_ref[rows, :] = jnp.sqrt(1.0 - a * a) * (g[:, w:2 * w] * xc_ref[rows, :])

    _row_chunks(n, lru_gates)

    lre = jnp.broadcast_to(lre_ref[...], (nb, ns))
    lim = jnp.broadcast_to(lim_ref[...], (nb, ns))

    def step(t, carry):
        hl, hre, him = carry
        i = pl.multiple_of(t * nb, nb)
        nre = lre * hre - lim * him + bu_ref[pl.ds(i, nb), 0:ns]
        nim = lre * him + lim * hre + bu_ref[pl.ds(i, nb), ns:2 * ns]
        bu_ref[pl.ds(i, nb), 0:ns] = nre
        bu_ref[pl.ds(i, nb), ns:2 * ns] = nim
        hl = a_ref[pl.ds(i, nb), :] * hl + u_ref[pl.ds(i, nb), :]
        u_ref[pl.ds(i, nb), :] = hl
        return hl, nre, nim

    hl, hre, him = lax.fori_loop(0, sb, step, (hl_ref[...], hre_ref[...], him_ref[...]), unroll=2)
    hl_ref[...] = hl
    hre_ref[...] = hre
    him_ref[...] = him

    for rows in halves:
        y_ref[rows, :] = _dot(bu_ref[rows, :].astype(BF16), cm_ref[...])

    def activations(r0):
        rows = pl.ds(r0, rc)
        lrows = pl.ds(r0 + halo, rc)
        lru_out = u_ref[rows, :] * _gelu_tanh(_pair(lslab_ref, 0, lrows))
        lslab_ref[0, lrows, :] = lru_out[:, 0:LANES]
        lslab_ref[1, lrows, :] = lru_out[:, LANES:2 * LANES]
        act_ref[rows, :] = _gelu_tanh(y_ref[rows, :] + d_ref[...] * _pair(sslab_ref, 0, rows))

    _row_chunks(n, activations)
    for rows in halves:
        y_ref[rows, :] = _dot(act_ref[rows, :].astype(BF16), gw_ref[...])

    def glu(r0):
        rows = pl.ds(r0, rc)
        out = act_ref[rows, :] * jax.nn.sigmoid(y_ref[rows, :] + gb_ref[...])
        sslab_ref[0, rows, :] = out[:, 0:LANES]
        sslab_ref[1, rows, :] = out[:, LANES:2 * LANES]

    _row_chunks(n, glu)

    for ks in (2, 3):
        lslab_ref[ks, 0:halo, :] = lslab_ref[ks, n:n + halo, :]
    _from_time_major(lslab_ref, yl_ref, w // LANES, row0=halo)
    _from_time_major(sslab_ref, ys_ref, w // LANES)


def _scan_mixers(zl3, zs3, lru_consts, s5_consts):
    nb, s, _ = zl3.shape
    sb = SCAN_BLOCK
    n = nb * sb
    halo = (CONV_W - 1) * nb
    assert nb == SUBLANES and n % (2 * ROW_CHUNK) == 0
    blk = lambda i: (0, i, 0)
    consts = tuple(lru_consts) + tuple(s5_consts)
    y_spec = pl.BlockSpec((nb, sb, MIX_W), blk)
    y_shape = jax.ShapeDtypeStruct((nb, s, MIX_W), BF16)
    return pl.pallas_call(
        _scan_body,
        grid=(s // sb,),
        in_specs=[pl.BlockSpec((nb, sb, 2 * MIX_W), blk), pl.BlockSpec((nb, sb, MIX_W), blk)]
                 + [_spec(a) for a in consts],
        out_specs=[y_spec, y_spec],
        out_shape=[y_shape, y_shape],
        scratch_shapes=[pltpu.VMEM((2 * MIX_W // LANES, halo + n, LANES), F32),
                        pltpu.VMEM((MIX_W // LANES, n, LANES), F32),
                        pltpu.VMEM((n, MIX_W), F32),
                        pltpu.VMEM((n, 2 * MIX_W), F32),
                        pltpu.VMEM((n, MIX_W), F32),
                        pltpu.VMEM((n, MIX_W), F32),
                        pltpu.VMEM((nb, MIX_W), F32),
                        pltpu.VMEM((n, 2 * S5_LANES), F32),
                        pltpu.VMEM((n, MIX_W), F32),
                        pltpu.VMEM((n, MIX_W), F32),
                        pltpu.VMEM((nb, S5_LANES), F32),
                        pltpu.VMEM((nb, S5_LANES), F32)],
        compiler_params=_params("arbitrary"),
        name="scan_mixers",
    )(zl3, zs3, *[_arr(a) for a in consts])


def _block_diag(blocks):
    g, r, c = blocks.shape[-3:]
    eye = jnp.eye(g, dtype=blocks.dtype)
    out = jnp.einsum('gh,...grc->...grhc', eye, blocks)
    return out.reshape(blocks.shape[:-3] + (g * r, g * c))


def _rotary_layout(w):
    lead = w.shape[:-1]
    w = w.reshape(lead + (HEADS, HEAD_W // 2, 2))
    return jnp.moveaxis(w, -1, -3).reshape(lead + (MIX_W,))


def _prep_w_rot(w_in_bf):
    base = 4 * MIX_W
    return jnp.concatenate([_rotary_layout(w_in_bf[..., base:base + MIX_W]),
                            _rotary_layout(w_in_bf[..., base + MIX_W:base + 2 * MIX_W])], axis=-1)


def _prep_s5(lam_re, lam_im, b_re, b_im, c_re, c_im, log_dt):
    step = jnp.exp(log_dt)[..., None]
    mag = jnp.exp(lam_re * step)
    lb_re = mag * jnp.cos(lam_im * step)
    lb_im = mag * jnp.sin(lam_im * step)
    den = lam_re * lam_re + lam_im * lam_im
    f_re = ((lb_re - 1.0) * lam_re + lb_im * lam_im) / den
    f_im = (lb_im * lam_re - (lb_re - 1.0) * lam_im) / den
    bb_re = f_re[..., None] * b_re - f_im[..., None] * b_im
    bb_im = f_re[..., None] * b_im + f_im[..., None] * b_re
    bd = jnp.concatenate([_block_diag(jnp.swapaxes(bb_re, -1, -2)),
                          _block_diag(jnp.swapaxes(bb_im, -1, -2))], axis=-1)
    cm = jnp.concatenate([_block_diag(jnp.swapaxes(c_re, -1, -2)),
                          -_block_diag(jnp.swapaxes(c_im, -1, -2))], axis=-2)
    n_layers = lam_re.shape[0]
    return (lb_re.reshape(n_layers, 1, -1), lb_im.reshape(n_layers, 1, -1),
            bd.astype(BF16), cm.astype(BF16))


def _rows(v):
    return v.reshape(v.shape[0], 1, -1).astype(F32)


def kernel(x, mem, hg_lower_bounds, norm_mix_pre, norm_mix_post, w_in, w_gate, b_gate, hg_norm, ret_norm, lru_conv_w, lru_conv_b, lru_wa, lru_ba, lru_wx, lru_bx, lru_lambda, s5_lam_re, s5_lam_im, s5_b_re, s5_b_im, s5_c_re, s5_c_im, s5_d, s5_log_dt, s5_glu_w, s5_glu_b, w_up, w_out, norm_xa_pre, norm_xa_post, norm_mem, xa_w_q, xa_w_kv, xa_w_o, norm_ffn_pre, norm_ffn_post, ffn_w_gu, ffn_w_down):
    b, s, d = x.shape
    depth = w_in.shape[0]
    t = b * s

    p = jax.nn.softmax(hg_lower_bounds.astype(F32), axis=0)
    lower_bounds = _rows(jnp.cumsum(p, axis=0) - p[0:1])
    w_in_bf = w_in.astype(BF16)
    w_rot = _prep_w_rot(w_in_bf)
    w_gate_bf, w_up_bf, w_out_bf = w_gate.astype(BF16), w_up.astype(BF16), w_out.astype(BF16)
    w_q_bf, w_kv_bf, w_o_bf = xa_w_q.astype(BF16), xa_w_kv.astype(BF16), xa_w_o.astype(BF16)
    w_gu_bf, w_down_bf = ffn_w_gu.astype(BF16), ffn_w_down.astype(BF16)
    glu_w_bf = s5_glu_w.astype(BF16)
    lru_wbd = jnp.concatenate([_block_diag(lru_wa), _block_diag(lru_wx)], axis=-1).astype(BF16)
    lru_bias = _rows(jnp.concatenate([lru_ba, lru_bx], axis=-1))
    lru_c = _rows(-LRU_C * jax.nn.softplus(-lru_lambda.astype(F32)))
    lre, lim, bd, cm = _prep_s5(s5_lam_re, s5_lam_im, s5_b_re, s5_b_im, s5_c_re, s5_c_im, s5_log_dt)
    rows = {name: _rows(v) for name, v in dict(
        mix_pre=norm_mix_pre, mix_post=norm_mix_post, b_gate=b_gate, hg_norm=hg_norm,
        ret_norm=ret_norm, conv_b=lru_conv_b, s5_d=s5_d, glu_b=s5_glu_b, xa_pre=norm_xa_pre,
        xa_post=norm_xa_post, mem=norm_mem, ffn_pre=norm_ffn_pre, ffn_post=norm_ffn_post).items()}
    conv_w = lru_conv_w.astype(F32)
    mem2 = mem.reshape(-1, d)

    x2 = x.reshape(t, d)
    for l in range(depth):
        at = lambda stacked: (stacked, l)
        z_hg, z_ret, z_lru, z_s5 = _in_proj(x2, at(rows["mix_pre"]), at(w_in_bf), at(w_rot))
        y_a = _hgrn2(z_hg.reshape(b, s, -1), at(lower_bounds), at(rows["hg_norm"]))
        y_b = _retention(z_ret.reshape(b, s, -1), at(rows["ret_norm"]))
        y_c, y_d = _scan_mixers(
            z_lru.reshape(b, s, -1), z_s5.reshape(b, s, -1),
            (at(conv_w), at(rows["conv_b"]), at(lru_wbd), at(lru_bias), at(lru_c)),
            (at(lre), at(lim), at(bd), at(cm), at(rows["s5_d"]), at(glu_w_bf), at(rows["glu_b"])))
        ys = [y.reshape(t, MIX_W) for y in (y_a, y_b, y_c, y_d)]
        x2 = _post(x2, ys, at(rows["mix_pre"]), at(rows["mix_post"]), at(w_gate_bf),
                   at(rows["b_gate"]), at(w_up_bf), at(w_out_bf))

        k2, v2 = _kv_proj(mem2, at(rows["mem"]), at(w_kv_bf))
        x2 = _xattn(x2, k2.reshape(b, -1, d), v2.reshape(b, -1, d),
                    at(rows["xa_pre"]), at(rows["xa_post"]), at(w_q_bf), at(w_o_bf))
        x2 = _ffn(x2, at(rows["ffn_pre"]), at(rows["ffn_post"]), at(w_gu_bf), at(w_down_bf))
    return x2.reshape(b, s, d)
```

```python
import functools
import math

import jax
import jax.numpy as jnp
from jax import lax
from jax.experimental import pallas as pl
from jax.experimental.pallas import tpu as pltpu

F32 = jnp.float32
BF16 = jnp.bfloat16

EPS = 1e-6
LANES = 128
SUBLANES = 8
MXU_DIM = 256
VMEM_LIMIT = 56 * 1024 * 1024

MIX_W = 256
HEADS = 4
HEAD_W = MIX_W // HEADS
HG_CHUNK = 32
SEQ_BLOCK = 256
HG_BATCH_GROUP = 2
RET_BATCH_GROUP = 4
ROPE_BASE = 10000.0
LRU_C = 8.0
CONV_W = 4
S5_GROUPS = 16
S5_P = 16
S5_N = 64
S5_LANES = S5_GROUPS * S5_N
ROW_TILE = 512
IN_TILE = 1024
POST_TILE = 1024
XATTN_TILE = 1024
FFN_TILE = 1024
FFN_CHUNKS = 3
SCAN_BLOCK = 128
ROW_CHUNK = 128


def _dot(a, b):
    return jnp.dot(a, b, preferred_element_type=F32)


def _dot_nt(a, b):
    return lax.dot_general(a, b, (((1,), (1,)), ((), ())), preferred_element_type=F32)


def _dot_tn(a, b):
    return lax.dot_general(a, b, (((0,), (0,)), ((), ())), preferred_element_type=F32)


def _rms(x, g):
    return x * lax.rsqrt(jnp.mean(x * x, axis=-1, keepdims=True) + EPS) * g


def _silu(x):
    return x * jax.nn.sigmoid(x)


def _gelu_tanh(x):
    return 0.5 * x * (1.0 + jnp.tanh(math.sqrt(2.0 / math.pi) * (x + 0.044715 * (x * x * x))))


def _split2(x):
    hi = x.astype(BF16)
    lo = (x - hi.astype(F32)).astype(BF16)
    return hi, lo


def _params(*sem):
    return pltpu.CompilerParams(dimension_semantics=sem, vmem_limit_bytes=VMEM_LIMIT)


def _spec(p):
    if isinstance(p, tuple):
        arr, layer = p
        shape = arr.shape[1:]
        return pl.BlockSpec((None,) + shape, lambda *_: (layer,) + (0,) * len(shape),
                            pipeline_mode=pl.Buffered(1))
    return pl.BlockSpec(p.shape, lambda *_: (0,) * p.ndim, pipeline_mode=pl.Buffered(1))


def _arr(p):
    return p[0] if isinstance(p, tuple) else p


def _in_proj_body(x_ref, g_ref, w_ref, wrot_ref, hg_ref, ret_ref, lru_ref, s5_ref):
    w = MIX_W
    h = _rms(x_ref[...], g_ref[...]).astype(BF16)
    hg_ref[...] = _dot(h, w_ref[:, 0:4 * w]).astype(BF16)
    ret_ref[:, 0:2 * w] = _dot(h, wrot_ref[...]).astype(BF16)
    ret_ref[:, 2 * w:4 * w] = _dot(h, w_ref[:, 6 * w:8 * w]).astype(BF16)
    lru_ref[...] = _dot(h, w_ref[:, 8 * w:10 * w]).astype(BF16)
    s5_ref[...] = _dot(h, w_ref[:, 10 * w:11 * w]).astype(BF16)


def _in_proj(x2, g, w, wrot):
    t, d = x2.shape
    widths = (4 * MIX_W, 4 * MIX_W, 2 * MIX_W, MIX_W)
    return pl.pallas_call(
        _in_proj_body,
        grid=(t // IN_TILE,),
        in_specs=[pl.BlockSpec((IN_TILE, d), lambda i: (i, 0)), _spec(g), _spec(w), _spec(wrot)],
        out_specs=[pl.BlockSpec((IN_TILE, wd), lambda i: (i, 0)) for wd in widths],
        out_shape=[jax.ShapeDtypeStruct((t, wd), BF16) for wd in widths],
        compiler_params=_params("arbitrary"),
        name="in_proj",
    )(x2, _arr(g), _arr(w), _arr(wrot))


def _post_body(x_ref, y0_ref, y1_ref, y2_ref, y3_ref, gpre_ref, gpost_ref, wg_ref, bg_ref,
               wup_ref, wout_ref, o_ref):
    x = x_ref[...]
    d = x.shape[-1]
    h = _rms(x, gpre_ref[...]).astype(BF16)
    blocks = []
    for c0 in range(0, d, MXU_DIM):
        acc = None
        for n, y_ref in enumerate((y0_ref, y1_ref, y2_ref, y3_ref)):
            cols = slice(n * d + c0, n * d + c0 + MXU_DIM)
            gate = jax.nn.sigmoid(_dot(h, wg_ref[:, cols]) + bg_ref[:, cols])
            term = gate * _dot(y_ref[...], wup_ref[n, :, c0:c0 + MXU_DIM])
            acc = term if acc is None else acc + term
        blocks.append(acc.astype(BF16))
    merged = jnp.concatenate(blocks, axis=1)
    out = _dot(merged, wout_ref[...])
    o_ref[...] = x + _rms(out, gpost_ref[...])


def _post(x2, ys, gpre, gpost, wg, bg, wup, wout):
    t, d = x2.shape
    row = lambda i: (i, 0)
    return pl.pallas_call(
        _post_body,
        grid=(t // POST_TILE,),
        in_specs=[pl.BlockSpec((POST_TILE, d), row)]
                 + [pl.BlockSpec((POST_TILE, MIX_W), row) for _ in ys]
                 + [_spec(a) for a in (gpre, gpost, wg, bg, wup, wout)],
        out_specs=pl.BlockSpec((POST_TILE, d), row),
        out_shape=jax.ShapeDtypeStruct((t, d), F32),
        compiler_params=_params("arbitrary"),
        name="mix_post",
    )(x2, *ys, *[_arr(a) for a in (gpre, gpost, wg, bg, wup, wout)])


def _kv_body(m_ref, g_ref, w_ref, k_ref, v_ref):
    d = m_ref.shape[-1]
    m = _rms(m_ref[...], g_ref[...]).astype(BF16)
    k_ref[...] = _dot(m, w_ref[:, 0:d]).astype(BF16)
    v_ref[...] = _dot(m, w_ref[:, d:2 * d]).astype(BF16)


def _kv_proj(mem2, g, w):
    t, d = mem2.shape
    tile = min(ROW_TILE, t)
    row = lambda i: (i, 0)
    return pl.pallas_call(
        _kv_body,
        grid=(t // tile,),
        in_specs=[pl.BlockSpec((tile, d), row), _spec(g), _spec(w)],
        out_specs=[pl.BlockSpec((tile, d), row), pl.BlockSpec((tile, d), row)],
        out_shape=[jax.ShapeDtypeStruct((t, d), BF16), jax.ShapeDtypeStruct((t, d), BF16)],
        compiler_params=_params("arbitrary"),
        name="kv_proj",
    )(mem2, _arr(g), _arr(w))


def _xattn_body(x_ref, k_ref, v_ref, gpre_ref, gpost_ref, wq_ref, wo_ref, o_ref):
    x = x_ref[...]
    d = x.shape[-1]
    dh = d // HEADS
    heads = [slice(hd * dh, (hd + 1) * dh) for hd in range(HEADS)]
    h = _rms(x, gpre_ref[...]).astype(BF16)
    q = _dot(h, wq_ref[...])
    scores = [_dot_nt(q[:, sl].astype(BF16), k_ref[:, sl]) * (dh ** -0.5) for sl in heads]
    probs = []
    for s in scores:
        p = jnp.exp(s - jnp.max(s, axis=-1, keepdims=True))
        probs.append((p / jnp.sum(p, axis=-1, keepdims=True)).astype(BF16))
    outs = [_dot(p, v_ref[:, sl]) for p, sl in zip(probs, heads)]
    o = jnp.concatenate(outs, axis=1).astype(BF16)
    o_ref[...] = x + _rms(_dot(o, wo_ref[...]), gpost_ref[...])


def _xattn(x2, k3, v3, gpre, gpost, wq, wo):
    t, d = x2.shape
    b, m, _ = k3.shape
    assert (t // b) % XATTN_TILE == 0, "a token tile must not straddle two batches' memories"
    per_batch = t // XATTN_TILE // b
    row = lambda i: (i, 0)
    mem = lambda i: (i // per_batch, 0, 0)
    return pl.pallas_call(
        _xattn_body,
        grid=(t // XATTN_TILE,),
        in_specs=[pl.BlockSpec((XATTN_TILE, d), row),
                  pl.BlockSpec((None, m, d), mem), pl.BlockSpec((None, m, d), mem)]
                 + [_spec(a) for a in (gpre, gpost, wq, wo)],
        out_specs=pl.BlockSpec((XATTN_TILE, d), row),
        out_shape=jax.ShapeDtypeStruct((t, d), F32),
        compiler_params=_params("arbitrary"),
        name="xattn",
    )(x2, k3, v3, *[_arr(a) for a in (gpre, gpost, wq, wo)])


def _ffn_body(ff_edges, x_ref, gpre_ref, gpost_ref, wgu_ref, wdn_ref, o_ref):
    x = x_ref[...]
    d_ff = wdn_ref.shape[0]
    h = _rms(x, gpre_ref[...]).astype(BF16)
    acc = None
    for c0, c1 in zip(ff_edges[:-1], ff_edges[1:]):
        gate = _dot(h, wgu_ref[:, c0:c1])
        up = _dot(h, wgu_ref[:, d_ff + c0:d_ff + c1])
        part = _dot((_silu(gate) * up).astype(BF16), wdn_ref[c0:c1, :])
        acc = part if acc is None else acc + part
    o_ref[...] = x + _rms(acc, gpost_ref[...])


def _ffn(x2, gpre, gpost, wgu, wdn):
    t, d = x2.shape
    d_ff = _arr(wdn).shape[-2]
    n_tiles = d_ff // MXU_DIM
    assert n_tiles * MXU_DIM == d_ff
    per_chunk = -(-n_tiles // FFN_CHUNKS)
    ff_edges = tuple(min(i * per_chunk, n_tiles) * MXU_DIM for i in range(FFN_CHUNKS + 1))
    row = lambda i: (i, 0)
    return pl.pallas_call(
        functools.partial(_ffn_body, ff_edges),
        grid=(t // FFN_TILE,),
        in_specs=[pl.BlockSpec((FFN_TILE, d), row)]
                 + [_spec(a) for a in (gpre, gpost, wgu, wdn)],
        out_specs=pl.BlockSpec((FFN_TILE, d), row),
        out_shape=jax.ShapeDtypeStruct((t, d), F32),
        compiler_params=_params("arbitrary"),
        name="ffn",
    )(x2, *[_arr(a) for a in (gpre, gpost, wgu, wdn)])


def _head_norm_gate(o, zg, ng, e_head):
    ms = _dot((o * o).astype(BF16), e_head)
    return o * lax.rsqrt(ms + EPS) * ng * _silu(zg)


def _hgrn2_body(z_ref, lb_ref, ng_ref, le_ref, eh_ref, y_ref, st_ref):
    @pl.when(pl.program_id(1) == 0)
    def _():
        st_ref[...] = jnp.zeros_like(st_ref)

    for bi in range(z_ref.shape[0]):
        _hgrn2_one(z_ref.at[bi], lb_ref, ng_ref, le_ref, eh_ref, y_ref.at[bi], st_ref.at[bi])


def _hgrn2_one(z_ref, lb_ref, ng_ref, le_ref, eh_ref, y_ref, st_ref):
    c = SEQ_BLOCK
    w = MIX_W
    n_chunks = c // HG_CHUNK

    zq = z_ref[:, 0:w].astype(F32)
    zf = z_ref[:, w:2 * w].astype(F32)
    v_bf = z_ref[:, 2 * w:3 * w]
    zg = z_ref[:, 3 * w:4 * w].astype(F32)
    lb = lb_ref[...]
    f = lb + (1.0 - lb) * jax.nn.sigmoid(zf)
    logf = jnp.log(f)
    k = 1.0 - f
    q = _silu(zq)

    le = le_ref[...]
    hi, lo = _split2(logf)
    b = _dot(le, hi) + _dot(le, lo)
    b_last = b.reshape(n_chunks, HG_CHUNK, w)[:, HG_CHUNK - 1:HG_CHUNK, :]
    b_end = jnp.broadcast_to(b_last, (n_chunks, HG_CHUNK, w)).reshape(c, w)
    q_dec = q * jnp.exp(b)
    k_inv = (k * jnp.exp(-b)).astype(BF16)
    k_end = (k * jnp.exp(b_end - b)).astype(BF16)

    lane = lax.broadcasted_iota(jnp.int32, (1, w), 1)
    row = lax.broadcasted_iota(jnp.int32, (c, c), 0)
    col = lax.broadcasted_iota(jnp.int32, (c, c), 1)
    shift = HG_CHUNK.bit_length() - 1
    causal = (row >= col) & ((row >> shift) == (col >> shift))
    hshift = HEAD_W.bit_length() - 1

    q_bf = q_dec.astype(BF16)
    o = None
    for hd in range(HEADS):
        mh = jnp.where((lane >> hshift) == hd, 1.0, 0.0).astype(BF16)
        a = jnp.where(causal, _dot_nt(q_bf * mh, k_inv), 0.0).astype(BF16)
        term = _dot(a, v_bf * mh)
        o = term if o is None else o + term

    st = st_ref[...]
    head_bd = (row >> hshift) == (col >> hshift)
    vt_bf = v_bf.astype(F32).T.astype(BF16)
    kvs = []
    for n in range(n_chunks):
        sl = slice(n * HG_CHUNK, (n + 1) * HG_CHUNK)
        pieces = [k_end[sl]]
        if n > 0:
            pieces.insert(0, jnp.zeros((n * HG_CHUNK, w), BF16))
        if (n + 1) * HG_CHUNK < c:
            pieces.append(jnp.zeros((c - (n + 1) * HG_CHUNK, w), BF16))
        kvs.append(_dot(vt_bf, jnp.concatenate(pieces, axis=0)))
    states = []
    for n in range(n_chunks):
        states.append(st.astype(BF16))
        dec = jnp.exp(b_end[n * HG_CHUNK:n * HG_CHUNK + 1, :])
        st = st * dec + jnp.where(head_bd, kvs[n], 0.0)
    st_ref[...] = st
    parts = []
    for n in range(n_chunks):
        sl = slice(n * HG_CHUNK, (n + 1) * HG_CHUNK)
        parts.append(o[sl] + _dot_nt(q_bf[sl], states[n]))
    o = jnp.concatenate(parts, axis=0)

    y_ref[...] = _head_norm_gate(o, zg, ng_ref[...], eh_ref[...]).astype(BF16)


def _chunk_sum_matrix(c):
    r = jnp.arange(c)
    same = (r[:, None] // HG_CHUNK) == (r[None, :] // HG_CHUNK)
    lower = same & (r[:, None] >= r[None, :])
    return lower.astype(BF16)


def _head_mean_matrix():
    r = jnp.arange(MIX_W)
    same = (r[:, None] // HEAD_W) == (r[None, :] // HEAD_W)
    return (same.astype(F32) / HEAD_W).astype(BF16)


def _hgrn2(z3, lb, ng):
    b, s, _ = z3.shape
    c = SEQ_BLOCK
    le = _chunk_sum_matrix(c)
    eh = _head_mean_matrix()
    nb = HG_BATCH_GROUP
    tok = lambda bi, i: (bi, i, 0)
    return pl.pallas_call(
        _hgrn2_body,
        grid=(b // nb, s // c),
        in_specs=[pl.BlockSpec((nb, c, 4 * MIX_W), tok)]
                 + [_spec(a) for a in (lb, ng, le, eh)],
        out_specs=pl.BlockSpec((nb, c, MIX_W), tok),
        out_shape=jax.ShapeDtypeStruct((b, s, MIX_W), BF16),
        scratch_shapes=[pltpu.VMEM((nb, MIX_W, MIX_W), F32)],
        compiler_params=_params("arbitrary", "arbitrary"),
        name="hgrn2",
    )(z3, *[_arr(a) for a in (lb, ng, le, eh)])


def _ret_body(z_ref, cos_ref, sin_ref, dec_ref, xi_ref, zeta_ref, gend_ref, ng_ref, eh_ref,
              y_ref, st_ref):
    @pl.when(pl.program_id(1) == 0)
    def _():
        st_ref[...] = jnp.zeros_like(st_ref)

    for bi in range(z_ref.shape[0]):
        _ret_one(z_ref.at[bi], cos_ref, sin_ref, dec_ref, xi_ref, zeta_ref, gend_ref, ng_ref, eh_ref,
                 y_ref.at[bi], st_ref.at[bi])


def _ret_one(z_ref, cos_ref, sin_ref, dec_ref, xi_ref, zeta_ref, gend_ref, ng_ref, eh_ref,
             y_ref, st_ref):
    w = MIX_W
    hw = w // 2

    cs = cos_ref[...]
    sn = sin_ref[...]
    q1 = z_ref[:, 0:hw].astype(F32)
    q2 = z_ref[:, hw:w].astype(F32)
    k1 = z_ref[:, w:w + hw].astype(F32)
    k2 = z_ref[:, w + hw:2 * w].astype(F32)
    v_bf = z_ref[:, 2 * w:3 * w]
    zg = z_ref[:, 3 * w:4 * w].astype(F32)
    qr = jnp.concatenate([q1 * cs - q2 * sn, q1 * sn + q2 * cs], axis=1)
    kr = jnp.concatenate([k1 * cs - k2 * sn, k1 * sn + k2 * cs], axis=1) * (HEAD_W ** -0.5)
    kr_bf = kr.astype(BF16)

    lane = lax.broadcasted_iota(jnp.int32, (1, w), 1)
    rshift = (hw // HEADS).bit_length() - 1
    hshift = HEAD_W.bit_length() - 1
    head_rot = (lane & (hw - 1)) >> rshift
    head_nat = lane >> hshift

    qr_bf = qr.astype(BF16)
    o = None
    for hd in range(HEADS):
        m_rot = jnp.where(head_rot == hd, 1.0, 0.0).astype(BF16)
        m_nat = jnp.where(head_nat == hd, 1.0, 0.0).astype(BF16)
        sc = (_dot_nt(qr_bf * m_rot, kr_bf) * dec_ref[hd]).astype(BF16)
        term = _dot(sc, v_bf * m_nat)
        o = term if o is None else o + term

    st = st_ref[...]
    o = o + _dot_nt((qr * xi_ref[...]).astype(BF16), st.astype(BF16))
    kv = _dot_tn(v_bf, (kr * zeta_ref[...]).astype(BF16))
    row = lax.broadcasted_iota(jnp.int32, (w, w), 0)
    col = lax.broadcasted_iota(jnp.int32, (w, w), 1)
    head_bd = (row >> hshift) == ((col & (hw - 1)) >> rshift)
    st_ref[...] = st * gend_ref[...] + jnp.where(head_bd, kv, 0.0)

    y_ref[...] = _head_norm_gate(o, zg, ng_ref[...], eh_ref[...]).astype(BF16)


def _rot_lane_head():
    lane = jnp.arange(MIX_W)
    return (lane % (MIX_W // 2)) // (MIX_W // 2 // HEADS)


def _retention(z3, ng):
    b, s, _ = z3.shape
    c = SEQ_BLOCK
    hw = MIX_W // 2
    n_freq = HEAD_W // 2
    pos = jnp.arange(s, dtype=F32)
    inv_freq = ROPE_BASE ** (-jnp.arange(0, HEAD_W, 2, dtype=F32) / HEAD_W)
    ang = pos[:, None] * jnp.tile(inv_freq, hw // n_freq)[None, :]
    cos, sin = jnp.cos(ang), jnp.sin(ang)
    log_gamma = jnp.log1p(-jnp.power(2.0, -5.0 - jnp.arange(HEADS, dtype=F32)))
    idx = jnp.arange(c, dtype=F32)
    rel = idx[:, None] - idx[None, :]
    causal = rel >= 0
    decay = jnp.where(causal, jnp.exp(jnp.where(causal, rel, 0.0)[None] * log_gamma[:, None, None]), 0.0)
    lg_lane = log_gamma[_rot_lane_head()]
    xi = jnp.exp((idx + 1.0)[:, None] * lg_lane[None, :])
    zeta = jnp.exp((c - 1.0 - idx)[:, None] * lg_lane[None, :])
    gend = jnp.exp(c * lg_lane)[None, :]
    eh = _head_mean_matrix()
    nb = RET_BATCH_GROUP
    tok = lambda bi, i: (bi, i, 0)
    return pl.pallas_call(
        _ret_body,
        grid=(b // nb, s // c),
        in_specs=[pl.BlockSpec((nb, c, 4 * MIX_W), tok),
                  pl.BlockSpec((c, hw), lambda bi, i: (i, 0)),
                  pl.BlockSpec((c, hw), lambda bi, i: (i, 0))]
                 + [_spec(a) for a in (decay, xi, zeta, gend, ng, eh)],
        out_specs=pl.BlockSpec((nb, c, MIX_W), tok),
        out_shape=jax.ShapeDtypeStruct((b, s, MIX_W), BF16),
        scratch_shapes=[pltpu.VMEM((nb, MIX_W, MIX_W), F32)],
        compiler_params=_params("arbitrary", "arbitrary"),
        name="retention",
    )(z3, cos, sin, *[_arr(a) for a in (decay, xi, zeta, gend, ng, eh)])


def _to_time_major(src_ref, slab_ref, n_slabs, row0=0):
    nb, sb, _ = src_ref.shape
    for bi in range(nb):
        for ks in range(n_slabs):
            slab_ref[ks, pl.ds(row0 + bi, sb, stride=nb), :] = (
                src_ref[bi, :, ks * LANES:(ks + 1) * LANES].astype(slab_ref.dtype))


def _from_time_major(slab_ref, dst_ref, n_slabs, row0=0):
    nb, sb, _ = dst_ref.shape
    for bi in range(nb):
        cols = [slab_ref[ks, pl.ds(row0 + bi, sb, stride=nb), :] for ks in range(n_slabs)]
        dst_ref[bi, :, :] = jnp.concatenate(cols, axis=1).astype(dst_ref.dtype)


def _row_chunks(n_rows, body):
    for r0 in range(0, n_rows, ROW_CHUNK):
        body(r0)


def _pair(slab_ref, k0, rows):
    return jnp.concatenate([slab_ref[k0, rows, :], slab_ref[k0 + 1, rows, :]], axis=1)


def _scan_body(zl_ref, zs_ref, cw_ref, cb_ref, wbd_ref, bias_ref, c_ref,
               lre_ref, lim_ref, bd_ref, cm_ref, d_ref, gw_ref, gb_ref, yl_ref, ys_ref,
               lslab_ref, sslab_ref, xc_ref, g_ref, a_ref, u_ref, hl_ref,
               bu_ref, y_ref, act_ref, hre_ref, him_ref):
    nb, sb, _ = zl_ref.shape
    n = nb * sb
    rc = ROW_CHUNK
    halves = (slice(0, n // 2), slice(n // 2, n))
    w = MIX_W
    ns = S5_LANES
    halo = (CONV_W - 1) * nb

    @pl.when(pl.program_id(0) == 0)
    def _():
        lslab_ref[:, 0:halo, :] = jnp.zeros((lslab_ref.shape[0], halo, LANES), F32)
        hl_ref[...] = jnp.zeros_like(hl_ref)
        hre_ref[...] = jnp.zeros_like(hre_ref)
        him_ref[...] = jnp.zeros_like(him_ref)

    _to_time_major(zs_ref, sslab_ref, w // LANES)
    _to_time_major(zl_ref, lslab_ref, 2 * w // LANES, row0=halo)

    for rows in halves:
        bu_ref[rows, :] = _dot(_pair(sslab_ref, 0, rows).astype(BF16), bd_ref[...])

    def lru_conv(r0):
        xc = cb_ref[...]
        for j in range(CONV_W):
            xc = xc + cw_ref[j:j + 1, :] * _pair(lslab_ref, 2, pl.ds(r0 + j * nb, rc))
        xc_ref[pl.ds(r0, rc), :] = xc

    _row_chunks(n, lru_conv)
    for rows in halves:
        g_ref[rows, :] = _dot(xc_ref[rows, :].astype(BF16), wbd_ref[...])

    def lru_gates(r0):
        rows = pl.ds(r0, rc)
        g = jax.nn.sigmoid(g_ref[rows, :] + bias_ref[...])
        a = jnp.exp(c_ref[...] * g[:, 0:w])
        a_ref[rows, :] = a
        u_ref[rows, :] = jnp.sqrt(1.0 - a * a) * (g[:, w:2 * w] * xc_ref[rows, :])

    _row_chunks(n, lru_gates)

    lre = jnp.broadcast_to(lre_ref[...], (nb, ns))
    lim = jnp.broadcast_to(lim_ref[...], (nb, ns))

    def step(t, carry):
        hl, hre, him = carry
        i = pl.multiple_of(t * nb, nb)
        nre = lre * hre - lim * him + bu_ref[pl.ds(i, nb), 0:ns]
        nim = lre * him + lim * hre + bu_ref[pl.ds(i, nb), ns:2 * ns]
        bu_ref[pl.ds(i, nb), 0:ns] = nre
        bu_ref[pl.ds(i, nb), ns:2 * ns] = nim
        hl = a_ref[pl.ds(i, nb), :] * hl + u_ref[pl.ds(i, nb), :]
        u_ref[pl.ds(i, nb), :] = hl
        return hl, nre, nim

    hl, hre, him = lax.fori_loop(0, sb, step, (hl_ref[...], hre_ref[...], him_ref[...]), unroll=2)
    hl_ref[...] = hl
    hre_ref[...] = hre
    him_ref[...] = him

    for rows in halves:
        y_ref[rows, :] = _dot(bu_ref[rows, :].astype(BF16), cm_ref[...])

    def activations(r0):
        rows = pl.ds(r0, rc)
        lrows = pl.ds(r0 + halo, rc)
        lru_out = u_ref[rows, :] * _gelu_tanh(_pair(lslab_ref, 0, lrows))
        lslab_ref[0, lrows, :] = lru_out[:, 0:LANES]
        lslab_ref[1, lrows, :] = lru_out[:, LANES:2 * LANES]
        act_ref[rows, :] = _gelu_tanh(y_ref[rows, :] + d_ref[...] * _pair(sslab_ref, 0, rows))

    _row_chunks(n, activations)
    for rows in halves:
        y_ref[rows, :] = _dot(act_ref[rows, :].astype(BF16), gw_ref[...])

    def glu(r0):
        rows = pl.ds(r0, rc)
        out = act_ref[rows, :] * jax.nn.sigmoid(y_ref[rows, :] + gb_ref[...])
        sslab_ref[0, rows, :] = out[:, 0:LANES]
        sslab_ref[1, rows, :] = out[:, LANES:2 * LANES]

    _row_chunks(n, glu)

    for ks in (2, 3):
        lslab_ref[ks, 0:halo, :] = lslab_ref[ks, n:n + halo, :]
    _from_time_major(lslab_ref, yl_ref, w // LANES, row0=halo)
    _from_time_major(sslab_ref, ys_ref, w // LANES)


def _scan_mixers(zl3, zs3, lru_consts, s5_consts):
    nb, s, _ = zl3.shape
    sb = SCAN_BLOCK
    n = nb * sb
    halo = (CONV_W - 1) * nb
    assert nb == SUBLANES and n % (2 * ROW_CHUNK) == 0
    blk = lambda i: (0, i, 0)
    consts = tuple(lru_consts) + tuple(s5_consts)
    y_spec = pl.BlockSpec((nb, sb, MIX_W), blk)
    y_shape = jax.ShapeDtypeStruct((nb, s, MIX_W), BF16)
    return pl.pallas_call(
        _scan_body,
        grid=(s // sb,),
        in_specs=[pl.BlockSpec((nb, sb, 2 * MIX_W), blk), pl.BlockSpec((nb, sb, MIX_W), blk)]
                 + [_spec(a) for a in consts],
        out_specs=[y_spec, y_spec],
        out_shape=[y_shape, y_shape],
        scratch_shapes=[pltpu.VMEM((2 * MIX_W // LANES, halo + n, LANES), F32),
                        pltpu.VMEM((MIX_W // LANES, n, LANES), F32),
                        pltpu.VMEM((n, MIX_W), F32),
                        pltpu.VMEM((n, 2 * MIX_W), F32),
                        pltpu.VMEM((n, MIX_W), F32),
                        pltpu.VMEM((n, MIX_W), F32),
                        pltpu.VMEM((nb, MIX_W), F32),
                        pltpu.VMEM((n, 2 * S5_LANES), F32),
                        pltpu.VMEM((n, MIX_W), F32),
                        pltpu.VMEM((n, MIX_W), F32),
                        pltpu.VMEM((nb, S5_LANES), F32),
                        pltpu.VMEM((nb, S5_LANES), F32)],
        compiler_params=_params("arbitrary"),
        name="scan_mixers",
    )(zl3, zs3, *[_arr(a) for a in consts])


def _block_diag(blocks):
    g, r, c = blocks.shape[-3:]
    eye = jnp.eye(g, dtype=blocks.dtype)
    out = jnp.einsum('gh,...grc->...grhc', eye, blocks)
    return out.reshape(blocks.shape[:-3] + (g * r, g * c))


def _rotary_layout(w):
    lead = w.shape[:-1]
    w = w.reshape(lead + (HEADS, HEAD_W // 2, 2))
    return jnp.moveaxis(w, -1, -3).reshape(lead + (MIX_W,))


def _prep_w_rot(w_in_bf):
    base = 4 * MIX_W
    return jnp.concatenate([_rotary_layout(w_in_bf[..., base:base + MIX_W]),
                            _rotary_layout(w_in_bf[..., base + MIX_W:base + 2 * MIX_W])], axis=-1)


def _prep_s5(lam_re, lam_im, b_re, b_im, c_re, c_im, log_dt):
    step = jnp.exp(log_dt)[..., None]
    mag = jnp.exp(lam_re * step)
    lb_re = mag * jnp.cos(lam_im * step)
    lb_im = mag * jnp.sin(lam_im * step)
    den = lam_re * lam_re + lam_im * lam_im
    f_re = ((lb_re - 1.0) * lam_re + lb_im * lam_im) / den
    f_im = (lb_im * lam_re - (lb_re - 1.0) * lam_im) / den
    bb_re = f_re[..., None] * b_re - f_im[..., None] * b_im
    bb_im = f_re[..., None] * b_im + f_im[..., None] * b_re
    bd = jnp.concatenate([_block_diag(jnp.swapaxes(bb_re, -1, -2)),
                          _block_diag(jnp.swapaxes(bb_im, -1, -2))], axis=-1)
    cm = jnp.concatenate([_block_diag(jnp.swapaxes(c_re, -1, -2)),
                          -_block_diag(jnp.swapaxes(c_im, -1, -2))], axis=-2)
    n_layers = lam_re.shape[0]
    return (lb_re.reshape(n_layers, 1, -1), lb_im.reshape(n_layers, 1, -1),
            bd.astype(BF16), cm.astype(BF16))


def _rows(v):
    return v.reshape(v.shape[0], 1, -1).astype(F32)


def kernel(x, mem, hg_lower_bounds, norm_mix_pre, norm_mix_post, w_in, w_gate, b_gate, hg_norm, ret_norm, lru_conv_w, lru_conv_b, lru_wa, lru_ba, lru_wx, lru_bx, lru_lambda, s5_lam_re, s5_lam_im, s5_b_re, s5_b_im, s5_c_re, s5_c_im, s5_d, s5_log_dt, s5_glu_w, s5_glu_b, w_up, w_out, norm_xa_pre, norm_xa_post, norm_mem, xa_w_q, xa_w_kv, xa_w_o, norm_ffn_pre, norm_ffn_post, ffn_w_gu, ffn_w_down):
    b, s, d = x.shape
    depth = w_in.shape[0]
    t = b * s

    p = jax.nn.softmax(hg_lower_bounds.astype(F32), axis=0)
    lower_bounds = _rows(jnp.cumsum(p, axis=0) - p[0:1])
    w_in_bf = w_in.astype(BF16)
    w_rot = _prep_w_rot(w_in_bf)
    w_gate_bf, w_up_bf, w_out_bf = w_gate.astype(BF16), w_up.astype(BF16), w_out.astype(BF16)
    w_q_bf, w_kv_bf, w_o_bf = xa_w_q.astype(BF16), xa_w_kv.astype(BF16), xa_w_o.astype(BF16)
    w_gu_bf, w_down_bf = ffn_w_gu.astype(BF16), ffn_w_down.astype(BF16)
    glu_w_bf = s5_glu_w.astype(BF16)
    lru_wbd = jnp.concatenate([_block_diag(lru_wa), _block_diag(lru_wx)], axis=-1).astype(BF16)
    lru_bias = _rows(jnp.concatenate([lru_ba, lru_bx], axis=-1))
    lru_c = _rows(-LRU_C * jax.nn.softplus(-lru_lambda.astype(F32)))
    lre, lim, bd, cm = _prep_s5(s5_lam_re, s5_lam_im, s5_b_re, s5_b_im, s5_c_re, s5_c_im, s5_log_dt)
    rows = {name: _rows(v) for name, v in dict(
        mix_pre=norm_mix_pre, mix_post=norm_mix_post, b_gate=b_gate, hg_norm=hg_norm,
        ret_norm=ret_norm, conv_b=lru_conv_b, s5_d=s5_d, glu_b=s5_glu_b, xa_pre=norm_xa_pre,
        xa_post=norm_xa_post, mem=norm_mem, ffn_pre=norm_ffn_pre, ffn_post=norm_ffn_post).items()}
    conv_w = lru_conv_w.astype(F32)
    mem2 = mem.reshape(-1, d)

    x2 = x.reshape(t, d)
    for l in range(depth):
        at = lambda stacked: (stacked, l)
        z_hg, z_ret, z_lru, z_s5 = _in_proj(x2, at(rows["mix_pre"]), at(w_in_bf), at(w_rot))
        y_a = _hgrn2(z_hg.reshape(b, s, -1), at(lower_bounds), at(rows["hg_norm"]))
        y_b = _retention(z_ret.reshape(b, s, -1), at(rows["ret_norm"]))
        y_c, y_d = _scan_mixers(
            z_lru.reshape(b, s, -1), z_s5.reshape(b, s, -1),
            (at(conv_w), at(rows["conv_b"]), at(lru_wbd), at(lru_bias), at(lru_c)),
            (at(lre), at(lim), at(bd), at(cm), at(rows["s5_d"]), at(glu_w_bf), at(rows["glu_b"])))
        ys = [y.reshape(t, MIX_W) for y in (y_a, y_b, y_c, y_d)]
        x2 = _post(x2, ys, at(rows["mix_pre"]), at(rows["mix_post"]), at(w_gate_bf),
                   at(rows["b_gate"]), at(w_up_bf), at(w_out_bf))

        k2, v2 = _kv_proj(mem2, at(rows["mem"]), at(w_kv_bf))
        x2 = _xattn(x2, k2.reshape(b, -1, d), v2.reshape(b, -1, d),
                    at(rows["xa_pre"]), at(rows["xa_post"]), at(w_q_bf), at(w_o_bf))
        x2 = _ffn(x2, at(rows["ffn_pre"]), at(rows["ffn_post"]), at(w_gu_bf), at(w_down_bf))
    return x2.reshape(b, s, d)
```
